```python
import jax
import jax.numpy as jnp
from jax import lax
import numpy as np

D_MODEL = 1024
BATCH = 8
SEQ = 2048
DEPTH = 4

CHUNK = 64
N_META = 16
NORM_EPS = 1e-6
D_MIX = D_MODEL

A_WIDTH = 3 * D_MODEL // 8
A_HEADS = 6
A_HEAD = A_WIDTH // A_HEADS
A_CONV = 4
A_C = 8.0
A_MIN_RAD = 0.9
A_MAX_RAD = 0.999

B_WIDTH = 3 * D_MODEL // 8
B_HEAD = 64
B_HEADS = B_WIDTH // B_HEAD
B_LORA_W = 32
B_LORA_A = 32
B_LORA_V = 16
B_LORA_G = 64
B_GN_EPS = 64e-5

C_WIDTH = D_MIX - A_WIDTH - B_WIDTH
C_HEADS = 4
C_HEAD_K = C_WIDTH // C_HEADS
C_HEAD_V = C_WIDTH // C_HEADS

A_COLS = 2 * A_WIDTH
B_COLS = 3 * B_WIDTH + B_LORA_W + B_LORA_A + B_LORA_G
C_COLS = 4 * C_WIDTH
N_IN = A_COLS + B_COLS + C_COLS
D_FF = 4 * D_MODEL

kernel_name = "hymba_style_rglru_rwkv7_hgrn2_trunk"

F32 = jnp.float32


def _split(t, sizes):
    return jnp.split(t, np.cumsum(sizes)[:-1].tolist(), axis=-1)


def rmsnorm(x, g, eps=NORM_EPS):
    xf = x.astype(F32)
    y = xf * lax.rsqrt(jnp.mean(xf * xf, axis=-1, keepdims=True) + eps)
    return (y * g.astype(F32)).astype(x.dtype)


def token_shift(p, mu):
    prev = jnp.pad(p, ((0, 0), (1, 0), (0, 0)))[:, :-1]
    return p + mu * (prev - p)


def causal_depthwise_conv(x, w, b):
    k_width, ch = w.shape
    y = lax.conv_general_dilated(x, w[:, None, :].astype(x.dtype), window_strides=(1,),
                                 padding=[(k_width - 1, 0)],
                                 dimension_numbers=('NWC', 'WIO', 'NWC'),
                                 feature_group_count=ch)
    return y + b


def rg_lru_group(xa, gate, conv_w, conv_b, w_r, b_r, w_i, b_i, lam):
    bn, L, _ = xa.shape
    u = causal_depthwise_conv(xa.astype(F32), conv_w.astype(F32), conv_b).astype(F32)
    uh = u.reshape(bn, L, A_HEADS, A_HEAD)
    r = jax.nn.sigmoid(jnp.einsum('blhi,hij->blhj', uh, w_r).reshape(bn, L, A_WIDTH) + b_r)
    i = jax.nn.sigmoid(jnp.einsum('blhi,hij->blhj', uh, w_i).reshape(bn, L, A_WIDTH) + b_i)
    log_a = -A_C * r * jax.nn.softplus(-lam)
    a = jnp.exp(log_a)
    drive = jnp.sqrt(-jnp.expm1(2.0 * log_a)) * (i * u)

    def combine(left, right):
        a_l, b_l = left
        a_r, b_rr = right
        return a_l * a_r, a_r * b_l + b_rr

    _, h = lax.associative_scan(combine, (a, drive), axis=1)
    return h * jax.nn.gelu(gate.astype(F32), approximate=True)


def rwkv7_scan(r, decay, k, v, kk, aa):
    bn, L, H, N = r.shape

    def step(S, inp):
        r_t, w_t, k_t, v_t, kk_t, a_t = inp
        sa = jnp.einsum('bhvk,bhk->bhv', S, -kk_t)
        S = (S * w_t[:, :, None, :] + sa[..., None] * (kk_t * a_t)[:, :, None, :]
             + v_t[..., None] * k_t[:, :, None, :])
        return S, jnp.einsum('bhvk,bhk->bhv', S, r_t)

    xs = tuple(jnp.moveaxis(t, 1, 0) for t in (r, decay, k, v, kk, aa))
    _, y = lax.scan(step, jnp.zeros((bn, H, N, N), F32), xs)
    return jnp.moveaxis(y, 0, 1)


def rwkv7_group(p, mu, w0, w2, a0, a2, g2, k_k, k_a, r_k, ln_g, ln_b, v_first, v0, v1, v2):
    bn, L, _ = p.shape
    p = token_shift(p.astype(F32), mu)
    r, k, v, wd, ad, gd = _split(p, [B_WIDTH, B_WIDTH, B_WIDTH, B_LORA_W, B_LORA_A, B_LORA_G])
    w_log = -jax.nn.softplus(-(w0 + jnp.tanh(wd) @ w2)) - 0.5
    decay = jnp.exp(-jnp.exp(w_log))
    if v_first is None:
        v_first = v
    else:
        v = v + (v_first - v) * jax.nn.sigmoid(v0 + (v @ v1) @ v2)
    aa = jax.nn.sigmoid(a0 + ad @ a2)
    g = jax.nn.sigmoid(gd) @ g2

    def heads(t):
        return t.reshape(bn, L, B_HEADS, B_HEAD)

    kk = heads(k * k_k)
    kk = kk / jnp.maximum(jnp.sqrt(jnp.sum(kk * kk, axis=-1, keepdims=True)), 1e-12)
    k = k * (1.0 + (aa - 1.0) * k_a)
    rh, kh, vh, ah = heads(r), heads(k), heads(v), heads(aa)
    y = rwkv7_scan(rh, heads(decay), kh, vh, kk, ah)
    mean = jnp.mean(y, axis=-1, keepdims=True)
    var = jnp.mean(jnp.square(y - mean), axis=-1, keepdims=True)
    y = ((y - mean) * lax.rsqrt(var + B_GN_EPS)).reshape(bn, L, B_WIDTH) * ln_g + ln_b
    y = y + (jnp.sum(rh * kh * r_k, axis=-1, keepdims=True) * vh).reshape(bn, L, B_WIDTH)
    return y * g, v_first


def hgrn2_chunked(q, k, v, logf):
    bn, L, H, dk = q.shape
    dv = v.shape[-1]
    n = L // CHUNK

    def to_chunks(t):
        return jnp.moveaxis(t.reshape(bn, n, CHUNK, H, t.shape[-1]), 1, 0)

    causal = jnp.tril(jnp.ones((CHUNK, CHUNK), bool))[None, :, :, None, None]

    def step(S, inp):
        qc, kc, vc, gc = inp
        b = jnp.cumsum(gc, axis=1)
        o_inter = jnp.einsum('bthk,bhkv->bthv', qc * jnp.exp(b), S)
        diff = b[:, :, None] - b[:, None, :]
        dec = jnp.exp(jnp.where(causal, diff, -jnp.inf))
        att = jnp.einsum('bthk,bshk,btshk->bhts', qc, kc, dec)
        o_intra = jnp.einsum('bhts,bshv->bthv', att, vc)
        b_last = b[:, -1]
        S = S * jnp.exp(b_last)[..., None] + jnp.einsum(
            'bshk,bshv->bhkv', kc * jnp.exp(b_last[:, None] - b), vc)
        return S, o_inter + o_intra

    xs = (to_chunks(q), to_chunks(k), to_chunks(v), to_chunks(logf))
    _, o = lax.scan(step, jnp.zeros((bn, H, dk, dv), F32), xs)
    return jnp.moveaxis(o, 0, 1).reshape(bn, L, H, dv)


def hgrn2_group(p, lb, norm_g):
    bn, L, _ = p.shape
    q, fz, i, og = _split(p.astype(F32), [C_WIDTH, C_WIDTH, C_WIDTH, C_WIDTH])
    f = lb + (1.0 - lb) * jax.nn.sigmoid(fz)
    logf = jnp.log(f)
    k = 1.0 - f
    q = jax.nn.silu(q)
    pad = (-L) % CHUNK

    def prep(t, d):
        return jnp.pad(t.reshape(bn, L, C_HEADS, d), ((0, 0), (pad, 0), (0, 0), (0, 0)))

    o = hgrn2_chunked(prep(q, C_HEAD_K), prep(k, C_HEAD_K), prep(i, C_HEAD_V),
                      prep(logf, C_HEAD_K))[:, pad:]
    o = o * lax.rsqrt(jnp.mean(o * o, axis=-1, keepdims=True) + NORM_EPS) * norm_g
    return o.reshape(bn, L, C_WIDTH) * jax.nn.silu(og)


def setup_inputs(seed: int = 0) -> dict:
    key = jax.random.key(seed)
    ks = iter(jax.random.split(key, 40))

    def nrm(shape, s):
        return s * jax.random.normal(next(ks), shape, F32)

    def unif(shape, lo, hi):
        return jax.random.uniform(next(ks), shape, F32, lo, hi)

    x = nrm((BATCH, SEQ, D_MODEL), 1.0)
    meta = nrm((N_META, D_MODEL), 1.0)
    mix_norm = 1.0 + nrm((DEPTH, D_MODEL), 0.02)
    w_in = nrm((DEPTH, D_MODEL, N_IN), D_MODEL ** -0.5)
    a_conv_w = nrm((DEPTH, A_CONV, A_WIDTH), A_CONV ** -0.5)
    a_conv_b = nrm((DEPTH, A_WIDTH), 0.02)
    a_w_r = nrm((DEPTH, A_HEADS, A_HEAD, A_HEAD), A_HEAD ** -0.5)
    a_b_r = nrm((DEPTH, A_WIDTH), 0.02)
    a_w_i = nrm((DEPTH, A_HEADS, A_HEAD, A_HEAD), A_HEAD ** -0.5)
    a_b_i = nrm((DEPTH, A_WIDTH), 0.02)
    a_pow = unif((DEPTH, A_WIDTH), A_MIN_RAD, A_MAX_RAD)
    a_rad = a_pow ** (1.0 / A_C)
    a_lambda = jnp.log(a_rad) - jnp.log1p(-a_rad)
    b_mu = unif((DEPTH, B_COLS), 0.2, 0.8)
    b_w0 = unif((DEPTH, B_WIDTH), -5.0, -1.0)
    b_w2 = nrm((DEPTH, B_LORA_W, B_WIDTH), 0.1)
    b_a0 = nrm((DEPTH, B_WIDTH), 0.1)
    b_a2 = nrm((DEPTH, B_LORA_A, B_WIDTH), B_LORA_A ** -0.5)
    b_g2 = nrm((DEPTH, B_LORA_G, B_WIDTH), B_LORA_G ** -0.5)
    b_k_k = 0.85 + nrm((DEPTH, B_WIDTH), 0.02)
    b_k_a = 1.0 + nrm((DEPTH, B_WIDTH), 0.02)
    b_r_k = nrm((DEPTH, B_HEADS, B_HEAD), 0.1)
    b_ln_g = 1.0 + nrm((DEPTH, B_WIDTH), 0.02)
    b_ln_b = nrm((DEPTH, B_WIDTH), 0.02)
    b_v0 = nrm((DEPTH - 1, B_WIDTH), 0.1)
    b_v1 = nrm((DEPTH - 1, B_WIDTH, B_LORA_V), B_WIDTH ** -0.5)
    b_v2 = nrm((DEPTH - 1, B_LORA_V, B_WIDTH), B_LORA_V ** -0.5)
    c_lb = nrm((DEPTH, C_WIDTH), 0.5)
    c_norm = 1.0 + nrm((DEPTH, C_HEAD_V), 0.02)
    w_out = nrm((DEPTH, D_MIX, D_MODEL), D_MIX ** -0.5)
    ffn_norm = 1.0 + nrm((DEPTH, D_MODEL), 0.02)
    w_up = nrm((DEPTH, D_MODEL, D_FF), D_MODEL ** -0.5)
    w_down = nrm((DEPTH, D_FF, D_MODEL), D_FF ** -0.5)
    final_norm = 1.0 + nrm((D_MODEL,), 0.02)
    return {"x": x, "meta": meta, "mix_norm": mix_norm, "w_in": w_in,
            "a_conv_w": a_conv_w, "a_conv_b": a_conv_b, "a_w_r": a_w_r, "a_b_r": a_b_r,
            "a_w_i": a_w_i, "a_b_i": a_b_i, "a_lambda": a_lambda,
            "b_mu": b_mu, "b_w0": b_w0, "b_w2": b_w2, "b_a0": b_a0, "b_a2": b_a2,
            "b_g2": b_g2, "b_k_k": b_k_k, "b_k_a": b_k_a, "b_r_k": b_r_k,
            "b_ln_g": b_ln_g, "b_ln_b": b_ln_b, "b_v0": b_v0, "b_v1": b_v1, "b_v2": b_v2,
            "c_lb": c_lb, "c_norm": c_norm, "w_out": w_out, "ffn_norm": ffn_norm,
            "w_up": w_up, "w_down": w_down, "final_norm": final_norm}


def reference(x, meta, mix_norm, w_in, a_conv_w, a_conv_b, a_w_r, a_b_r, a_w_i, a_b_i, a_lambda,
              b_mu, b_w0, b_w2, b_a0, b_a2, b_g2, b_k_k, b_k_a, b_r_k, b_ln_g, b_ln_b,
              b_v0, b_v1, b_v2, c_lb, c_norm, w_out, ffn_norm, w_up, w_down, final_norm):
    bn = x.shape[0]
    dt = x.dtype
    h = jnp.concatenate([jnp.broadcast_to(meta[None].astype(dt), (bn, N_META, D_MODEL)), x], axis=1)
    lb_all = jnp.cumsum(jax.nn.softmax(c_lb.astype(F32), axis=0), axis=0)
    lb_all = lb_all - lb_all[0]
    v_first = None
    for l in range(DEPTH):
        hn = rmsnorm(h, mix_norm[l])
        proj = jnp.einsum('bld,dn->bln', hn, w_in[l])
        pa, pb, pc = _split(proj, [A_COLS, B_COLS, C_COLS])
        xa, ga = _split(pa, [A_WIDTH, A_WIDTH])
        y_a = rg_lru_group(xa, ga, a_conv_w[l], a_conv_b[l], a_w_r[l], a_b_r[l],
                           a_w_i[l], a_b_i[l], a_lambda[l])
        if l == 0:
            v0, v1, v2 = None, None, None
        else:
            v0, v1, v2 = b_v0[l - 1], b_v1[l - 1], b_v2[l - 1]
        y_b, v_first = rwkv7_group(pb, b_mu[l], b_w0[l], b_w2[l], b_a0[l], b_a2[l], b_g2[l],
                                   b_k_k[l], b_k_a[l], b_r_k[l], b_ln_g[l], b_ln_b[l],
                                   v_first, v0, v1, v2)
        y_c = hgrn2_group(pc, lb_all[l], c_norm[l])
        y = jnp.concatenate([y_a, y_b, y_c], axis=-1).astype(dt)
        h = (h + jnp.einsum('bln,nd->bld', y, w_out[l])).astype(dt)
        hn = rmsnorm(h, ffn_norm[l])
        u = jnp.einsum('bld,df->blf', hn, w_up[l])
        h = (h + jnp.einsum('blf,fd->bld', jnp.square(jax.nn.relu(u)), w_down[l])).astype(dt)
    return rmsnorm(h, final_norm)[:, N_META:]
```

```python
import functools

import numpy as np
import jax
import jax.numpy as jnp
from jax import lax
from jax.experimental import pallas as pl
from jax.experimental.pallas import tpu as pltpu

F32 = jnp.float32
BF16 = jnp.bfloat16

N_META = 16
NORM_EPS = 1e-6
CHUNK = 64

A_WIDTH = 384
A_HEADS = 6
A_C = 8.0
B_WIDTH = 384
B_HEAD = 64
B_HEADS = 6
B_LORA = 128
B_COLS = 3 * B_WIDTH + B_LORA
B_GN_EPS = 64e-5
C_WIDTH = 256
C_HEADS = 4
C_HEAD = 64
C_LEVELS = 6
A_OFF = 0
B_OFF = 2 * A_WIDTH
C_OFF = B_OFF + B_COLS
N_IN = C_OFF + 4 * C_WIDTH
D_MIX = A_WIDTH + B_WIDTH + C_WIDTH

V7X_VMEM_LIMIT = 56 * 1024 * 1024
ROW_TILE = 512


def _dot(a, b, ca=1, cb=0, hi=False):
    dims = (((ca,), (cb,)), ((), ()))
    if hi:
        return lax.dot_general(a, b, dims, precision=lax.Precision.HIGHEST,
                               preferred_element_type=F32)
    return lax.dot_general(a.astype(BF16), b.astype(BF16), dims, preferred_element_type=F32)


def _sigmoid(x):
    return jax.nn.sigmoid(x)


def _softplus(x):
    return jnp.maximum(x, 0.0) + jnp.log1p(jnp.exp(-jnp.abs(x)))


def _rmsnorm(x, g):
    return x * lax.rsqrt(jnp.mean(x * x, axis=-1, keepdims=True) + NORM_EPS) * g


def _proj_kernel(h_ref, g_ref, w_ref, o_ref):
    xn = _rmsnorm(h_ref[...], g_ref[...])
    o_ref[...] = jnp.dot(xn.astype(BF16), w_ref[...], preferred_element_type=F32)


def _proj(h2d, g, w_bf16):
    rows, d = h2d.shape
    n = w_bf16.shape[1]
    return pl.pallas_call(
        _proj_kernel,
        grid=(rows // ROW_TILE,),
        in_specs=[pl.BlockSpec((ROW_TILE, d), lambda i: (i, 0)),
                  pl.BlockSpec((1, d), lambda i: (0, 0)),
                  pl.BlockSpec((d, n), lambda i: (0, 0))],
        out_specs=pl.BlockSpec((ROW_TILE, n), lambda i: (i, 0)),
        out_shape=jax.ShapeDtypeStruct((rows, n), F32),
        compiler_params=pltpu.CompilerParams(dimension_semantics=("arbitrary",),
                                             vmem_limit_bytes=V7X_VMEM_LIMIT),
        name="proj",
    )(h2d, g, w_bf16)


def _post_kernel(final, f_chunk, h_ref, y_ref, wo_ref, g_ref, wu_ref, wd_ref, gf_ref, o_ref):
    h1 = h_ref[...] + jnp.dot(y_ref[...].astype(BF16), wo_ref[...], preferred_element_type=F32)
    hb = _rmsnorm(h1, g_ref[...]).astype(BF16)
    acc = h1
    d_ff = wu_ref.shape[1]
    for c in range(d_ff // f_chunk):
        u = jnp.dot(hb, wu_ref[:, c * f_chunk:(c + 1) * f_chunk], preferred_element_type=F32)
        a = jnp.square(jnp.maximum(u, 0.0)).astype(BF16)
        acc = acc + jnp.dot(a, wd_ref[c * f_chunk:(c + 1) * f_chunk, :],
                            preferred_element_type=F32)
    if final:
        acc = _rmsnorm(acc, gf_ref[...])
    o_ref[...] = acc


def _post(h2d, y2d, wo, g, wu, wd, gf, final):
    rows, d = h2d.shape
    d_ff = wu.shape[1]
    tile = ROW_TILE // 2
    const = lambda i: (0, 0)
    return pl.pallas_call(
        functools.partial(_post_kernel, final, 1024),
        grid=(rows // tile,),
        in_specs=[pl.BlockSpec((tile, d), lambda i: (i, 0)),
                  pl.BlockSpec((tile, D_MIX), lambda i: (i, 0)),
                  pl.BlockSpec((D_MIX, d), const),
                  pl.BlockSpec((1, d), const),
                  pl.BlockSpec((d, d_ff), const),
                  pl.BlockSpec((d_ff, d), const),
                  pl.BlockSpec((1, d), const)],
        out_specs=pl.BlockSpec((tile, d), lambda i: (i, 0)),
        out_shape=jax.ShapeDtypeStruct((rows, d), F32),
        compiler_params=pltpu.CompilerParams(dimension_semantics=("arbitrary",),
                                             vmem_limit_bytes=V7X_VMEM_LIMIT),
        name="post",
    )(h2d, y2d, wo, g, wu, wd, gf)


def _group_a(p_ref, cw_ref, cb_ref, wri_ref, bri_ref, lam_ref, xbuf, h_a):
    t = CHUNK
    xa = p_ref[0, :, A_OFF:A_OFF + A_WIDTH]
    ga = p_ref[0, :, A_OFF + A_WIDTH:A_OFF + 2 * A_WIDTH]
    xbuf[8:8 + t, :] = xa
    u = cb_ref[...]
    for j in range(4):
        u = u + cw_ref[j:j + 1, :] * xbuf[5 + j:5 + j + t, :]
    xbuf[0:8, :] = xbuf[t:t + 8, :]
    ri = _dot(u, wri_ref[...]) + bri_ref[...]
    r = _sigmoid(ri[:, :A_WIDTH])
    i = _sigmoid(ri[:, A_WIDTH:])
    log_a = (-A_C * r) * _softplus(-lam_ref[...])
    a = jnp.exp(log_a)
    d = jnp.sqrt(1.0 - a * a) * (i * u)
    rows = lax.broadcasted_iota(jnp.int32, (t, A_WIDTH), 0)
    s = 1
    while s < t:
        a_sh = jnp.where(rows < s, 1.0, pltpu.roll(a, s, 0))
        d_sh = jnp.where(rows < s, 0.0, pltpu.roll(d, s, 0))
        d = a * d_sh + d
        a = a * a_sh
        s *= 2
    hseq = d + a * h_a[0:1, :]
    h_a[0:1, :] = hseq[t - 1:t, :]
    return hseq * jax.nn.gelu(ga, approximate=True)


def _group_c(p_ref, lb_ref, cn_ref, tril, wlev, bo, s_c):
    t = CHUNK
    q_raw = p_ref[0, :, C_OFF:C_OFF + C_WIDTH]
    fz = p_ref[0, :, C_OFF + C_WIDTH:C_OFF + 2 * C_WIDTH]
    v = p_ref[0, :, C_OFF + 2 * C_WIDTH:C_OFF + 3 * C_WIDTH]
    og = p_ref[0, :, C_OFF + 3 * C_WIDTH:C_OFF + 4 * C_WIDTH]
    lb = lb_ref[...]
    f = lb + (1.0 - lb) * _sigmoid(fz)
    g = jnp.log(f)
    kd = 1.0 - f
    q = q_raw * _sigmoid(q_raw)
    b = _dot(tril, g, hi=True)
    e_lev = jnp.exp(_dot(wlev, g, hi=True))
    b_last = b[t - 1:t, :]
    qe = q * jnp.exp(b)
    kl = kd * jnp.exp(b_last - b)
    g_last = jnp.exp(b_last)
    dq = _dot(q * kd, bo, hi=True)
    ti = lax.broadcasted_iota(jnp.int32, (t, t), 0)
    si = lax.broadcasted_iota(jnp.int32, (t, t), 1)
    xs = jnp.where(ti > si, ti ^ si, 0)
    outs = []
    for h in range(C_HEADS):
        sl = slice(h * C_HEAD, (h + 1) * C_HEAD)
        att = jnp.zeros((t, t), F32)
        for l in range(C_LEVELS):
            el = e_lev[l * t:(l + 1) * t, sl]
            att = att + jnp.where((xs >> l) == 1, _dot(q[:, sl] * el, kd[:, sl] * el, 1, 1), 0.0)
        st = s_c[h]
        o = _dot(att, v[:, sl]) + _dot(qe[:, sl], st, 1, 1) + dq[:, sl] * v[:, sl]
        s_c[h] = st * g_last[:, sl] + _dot(v[:, sl], kl[:, sl], 0, 0)
        outs.append(o)
    o = jnp.concatenate(outs, axis=1)
    ms = _dot(o * o, bo, hi=True) * (1.0 / C_HEAD)
    return o * lax.rsqrt(ms + NORM_EPS) * cn_ref[...] * (og * _sigmoid(og))


def _group_b(p_ref, vf_ref, vfo_ref, mu_ref, wl_ref, w0_ref, a0_ref, kk_ref, ka_ref, rk_ref,
             lng_ref, lnb_ref, v0_ref, v1_ref, v2_ref, tril, bo, pbuf, s_b):
    t = CHUNK
    w = B_WIDTH
    pb = p_ref[0, :, B_OFF:B_OFF + B_COLS]
    pbuf[8:8 + t, :] = pb
    prev = pbuf[7:7 + t, :]
    pbuf[0:8, :] = pbuf[t:t + 8, :]
    ps = pb + mu_ref[...] * (prev - pb)
    r = ps[:, 0:w]
    k = ps[:, w:2 * w]
    v = ps[:, 2 * w:3 * w]
    lo = ps[:, 3 * w:3 * w + B_LORA]
    lane = lax.broadcasted_iota(jnp.int32, (t, B_LORA), 1)
    xl = jnp.where(lane < 32, jnp.tanh(lo), jnp.where(lane < 64, lo, _sigmoid(lo)))
    lora = _dot(xl, wl_ref[...])
    w_log = -_softplus(-(w0_ref[...] + lora[:, 0:w])) - 0.5
    lw = -jnp.exp(w_log)
    if vf_ref is None:
        vfo_ref[0] = v
    else:
        mix = _sigmoid(v0_ref[...] + _dot(_dot(v, v1_ref[...]), v2_ref[...]))
        v = v + (vf_ref[0] - v) * mix
    aa = _sigmoid(a0_ref[...] + lora[:, w:2 * w])
    gate = lora[:, 2 * w:3 * w]
    kk = k * kk_ref[...]
    kk = kk / jnp.maximum(jnp.sqrt(_dot(kk * kk, bo, hi=True)), 1e-12)
    k = k * (1.0 + (aa - 1.0) * ka_ref[...])
    ap = -kk
    bb = kk * aa
    c = _dot(tril, lw, hi=True)
    c_last = c[t - 1:t, :]
    a_h = ap * jnp.exp(c - lw)
    r_h = r * jnp.exp(c)
    e_inv = jnp.exp(-c)
    b_h = bb * e_inv
    k_h = k * e_inv
    e_cl = jnp.exp(c_last - c)
    b_c = bb * e_cl
    k_c = k * e_cl
    gam = jnp.exp(c_last)

    ti = lax.broadcasted_iota(jnp.int32, (t, t), 0)
    si = lax.broadcasted_iota(jnp.int32, (t, t), 1)
    strict = ti > si
    lower = ti >= si
    blk = (ti >> 4) == (si >> 4)
    eye = jnp.where(ti == si, 1.0, 0.0)
    ys = []
    for h in range(B_HEADS):
        sl = slice(h * B_HEAD, (h + 1) * B_HEAD)
        ar = jnp.concatenate([a_h[:, sl], r_h[:, sl]], axis=0)
        m_b = _dot(ar, b_h[:, sl], 1, 1)
        m_k = _dot(ar, k_h[:, sl], 1, 1)
        m_ab = jnp.where(strict, m_b[:t], 0.0)
        m_ak = jnp.where(strict, m_k[:t], 0.0)
        n_rb = jnp.where(lower, m_b[t:], 0.0)
        n_rk = jnp.where(lower, m_k[t:], 0.0)
        dg = jnp.where(blk, m_ab, 0.0)
        off = m_ab - dg
        d2 = _dot(dg, dg)
        d4 = _dot(d2, d2)
        d8 = _dot(d4, d4)
        x = eye + dg + d2 + _dot(dg, d2)
        x = x + _dot(x, d4)
        dinv = x + _dot(x, d8)
        gm = _dot(dinv, off)
        g2 = _dot(gm, gm)
        tinv = _dot(eye + gm + g2 + _dot(gm, g2), dinv)
        vh = v[:, sl]
        pm = _dot(tinv, a_h[:, sl])
        qm = _dot(tinv, _dot(m_ak, vh))
        p2 = r_h[:, sl] + _dot(n_rb, pm)
        q2 = _dot(n_rb, qm) + _dot(n_rk, vh)
        st = s_b[h]
        u = _dot(pm, st, 1, 1) + qm
        ys.append(_dot(p2, st, 1, 1) + q2)
        s_b[h] = st * gam[:, sl] + _dot(jnp.concatenate([u, vh], axis=0),
                                        jnp.concatenate([b_c[:, sl], k_c[:, sl]], axis=0), 0, 0)
    y = jnp.concatenate(ys, axis=1)
    inv_n = 1.0 / B_HEAD
    mean = _dot(y, bo, hi=True) * inv_n
    yc = y - mean
    var = _dot(yc * yc, bo, hi=True) * inv_n
    y = yc * lax.rsqrt(var + B_GN_EPS) * lng_ref[...] + lnb_ref[...]
    y = y + _dot(r * k * rk_ref[...], bo, hi=True) * v
    return y * gate


def _mixer_kernel(layer0, *refs):
    it = iter(refs)
    p_ref = next(it)
    vf_ref = None if layer0 else next(it)
    cw, cb, wri, bri, lam = (next(it) for _ in range(5))
    mu, wl, w0, a0, kkp, kap, rkp, lng, lnb = (next(it) for _ in range(9))
    v0 = v1 = v2 = None
    if not layer0:
        v0, v1, v2 = (next(it) for _ in range(3))
    lb, cn, tril_ref, wlev_ref, bo_ref = (next(it) for _ in range(5))
    y_ref = next(it)
    vfo_ref = next(it) if layer0 else None
    xbuf, h_a, pbuf, s_b, s_c = (next(it) for _ in range(5))

    @pl.when(pl.program_id(1) == 0)
    def _():
        xbuf[0:8, :] = jnp.zeros((8, A_WIDTH), F32)
        pbuf[0:8, :] = jnp.zeros((8, B_COLS), F32)
        h_a[...] = jnp.zeros_like(h_a)
        s_b[...] = jnp.zeros_like(s_b)
        s_c[...] = jnp.zeros_like(s_c)

    tril = tril_ref[...]
    bo = bo_ref[...]
    y_ref[0, :, 0:A_WIDTH] = _group_a(p_ref, cw, cb, wri, bri, lam, xbuf, h_a)
    y_ref[0, :, A_WIDTH:A_WIDTH + B_WIDTH] = _group_b(
        p_ref, vf_ref, vfo_ref, mu, wl, w0, a0, kkp, kap, rkp, lng, lnb, v0, v1, v2,
        tril, bo, pbuf, s_b)
    y_ref[0, :, A_WIDTH + B_WIDTH:D_MIX] = _group_c(
        p_ref, lb, cn, tril, wlev_ref[...], bo[:C_WIDTH, :C_WIDTH], s_c)


def _level_matrix():
    t = CHUNK
    w = np.zeros((C_LEVELS * t, t), np.float32)
    for l in range(C_LEVELS):
        half = 1 << l
        for row in range(t):
            base = (row >> (l + 1)) << (l + 1)
            if (row >> l) & 1:
                w[l * t + row, base + half:row + 1] = 1.0
            else:
                w[l * t + row, row + 1:base + half] = 1.0
    return w


def _mixer(proj, v_first, params, consts, layer0):
    bsz, lp, _ = proj.shape
    t = CHUNK
    const = lambda b, c: (0, 0)
    row = lambda b, c: (b, c, 0)
    ins = [proj]
    specs = [pl.BlockSpec((1, t, N_IN), row)]
    if not layer0:
        ins.append(v_first)
        specs.append(pl.BlockSpec((1, t, B_WIDTH), row))
    for arr in list(params) + list(consts):
        ins.append(arr)
        specs.append(pl.BlockSpec(arr.shape, const))
    out_shape = [jax.ShapeDtypeStruct((bsz, lp, D_MIX), F32)]
    out_specs = [pl.BlockSpec((1, t, D_MIX), row)]
    if layer0:
        out_shape.append(jax.ShapeDtypeStruct((bsz, lp, B_WIDTH), F32))
        out_specs.append(pl.BlockSpec((1, t, B_WIDTH), row))
    res = pl.pallas_call(
        functools.partial(_mixer_kernel, layer0),
        grid=(bsz, lp // t),
        in_specs=specs,
        out_specs=out_specs,
        out_shape=out_shape,
        scratch_shapes=[pltpu.VMEM((t + 8, A_WIDTH), F32),
                        pltpu.VMEM((8, A_WIDTH), F32),
                        pltpu.VMEM((t + 8, B_COLS), F32),
                        pltpu.VMEM((B_HEADS, B_HEAD, B_HEAD), F32),
                        pltpu.VMEM((C_HEADS, C_HEAD, C_HEAD), F32)],
        compiler_params=pltpu.CompilerParams(dimension_semantics=("arbitrary", "arbitrary"),
                                             vmem_limit_bytes=V7X_VMEM_LIMIT),
        name="mixer0" if layer0 else "mixer",
    )(*ins)
    return res if layer0 else (res[0], v_first)


def _block_diag(w):
    hn, n, _ = w.shape
    out = jnp.zeros((hn * n, hn * n), w.dtype)
    for h in range(hn):
        out = out.at[h * n:(h + 1) * n, h * n:(h + 1) * n].set(w[h])
    return out


def kernel(x, meta, mix_norm, w_in, a_conv_w, a_conv_b, a_w_r, a_b_r, a_w_i, a_b_i, a_lambda, b_mu, b_w0, b_w2, b_a0, b_a2, b_g2, b_k_k, b_k_a, b_r_k, b_ln_g, b_ln_b, b_v0, b_v1, b_v2, c_lb, c_norm, w_out, ffn_norm, w_up, w_down, final_norm):
    bsz, seq, d = x.shape
    depth = w_in.shape[0]
    l_real = N_META + seq
    lp = -(-l_real // CHUNK) * CHUNK
    while (bsz * lp) % ROW_TILE:
        lp += CHUNK
    h = jnp.concatenate([jnp.broadcast_to(meta[None].astype(x.dtype), (bsz, N_META, d)), x,
                         jnp.zeros((bsz, lp - l_real, d), x.dtype)], axis=1)
    h = h.reshape(bsz * lp, d)

    lb_all = jnp.cumsum(jax.nn.softmax(c_lb.astype(F32), axis=0), axis=0)
    lb_all = lb_all - lb_all[0]
    tril = jnp.asarray(np.tril(np.ones((CHUNK, CHUNK), np.float32)))
    wlev = jnp.asarray(_level_matrix())
    hid = np.arange(B_WIDTH) // B_HEAD
    bo = jnp.asarray((hid[:, None] == hid[None, :]).astype(np.float32))
    consts = (tril, wlev, bo)
    row = lambda a: a.reshape(1, -1)

    v_first = None
    for l in range(depth):
        proj = _proj(h, row(mix_norm[l]), w_in[l].astype(BF16)).reshape(bsz, lp, N_IN)
        wri = jnp.concatenate([_block_diag(a_w_r[l]), _block_diag(a_w_i[l])], axis=1).astype(BF16)
        bri = jnp.concatenate([a_b_r[l], a_b_i[l]]).reshape(1, -1)
        wl = jnp.zeros((B_LORA, 3 * B_WIDTH), F32)
        wl = wl.at[0:32, 0:B_WIDTH].set(b_w2[l])
        wl = wl.at[32:64, B_WIDTH:2 * B_WIDTH].set(b_a2[l])
        wl = wl.at[64:128, 2 * B_WIDTH:].set(b_g2[l])
        params = [a_conv_w[l], row(a_conv_b[l]), wri, bri, row(a_lambda[l]),
                  row(b_mu[l]), wl.astype(BF16), row(b_w0[l]), row(b_a0[l]), row(b_k_k[l]),
                  row(b_k_a[l]), row(b_r_k[l]), row(b_ln_g[l]), row(b_ln_b[l])]
        if l > 0:
            params += [row(b_v0[l - 1]), b_v1[l - 1].astype(BF16), b_v2[l - 1].astype(BF16)]
        params += [row(lb_all[l]), row(jnp.tile(c_norm[l], C_HEADS))]
        y, v_first = _mixer(proj, v_first, params, consts, l == 0)
        h = _post(h, y.reshape(bsz * lp, D_MIX), w_out[l].astype(BF16), row(ffn_norm[l]),
                  w_up[l].astype(BF16), w_down[l].astype(BF16), row(final_norm), l == depth - 1)
    return h.reshape(bsz, lp, d)[:, N_META:l_real]
```

```python
import functools

import numpy as np
import jax
import jax.numpy as jnp
from jax import lax
from jax.experimental import pallas as pl
from jax.experimental.pallas import tpu as pltpu

F32 = jnp.float32
BF16 = jnp.bfloat16

N_META = 16
NORM_EPS = 1e-6
CHUNK = 64
SUB = 3
STEP = SUB * CHUNK

A_WIDTH = 384
A_HEADS = 6
A_C = 8.0
B_WIDTH = 384
B_HEAD = 64
B_HEADS = 6
B_LORA = 128
B_COLS = 3 * B_WIDTH + B_LORA
B_GN_EPS = 64e-5
C_WIDTH = 256
C_HEADS = 4
C_HEAD = 64
C_LEVELS = 6
PAIR = 128
A_OFF = 0
B_OFF = 2 * A_WIDTH
C_OFF = B_OFF + B_COLS
N_IN = C_OFF + 4 * C_WIDTH
D_MIX = A_WIDTH + B_WIDTH + C_WIDTH

V7X_VMEM_LIMIT = 56 * 1024 * 1024
ROW_TILE = 512


def _dot(a, b, ca=1, cb=0):
    dims = (((ca,), (cb,)), ((), ()))
    return lax.dot_general(a.astype(BF16), b.astype(BF16), dims, preferred_element_type=F32)


def _split2(x):
    hi = x.astype(BF16)
    return hi, (x - hi.astype(F32)).astype(BF16)


def _sel_left(m, x):
    n = x.shape[1]
    r = jnp.dot(m, jnp.concatenate(_split2(x), axis=1), preferred_element_type=F32)
    return r[:, :n] + r[:, n:]


def _bd(y, first):
    return jnp.concatenate([jnp.where(first, y, 0.0), jnp.where(first, 0.0, y)], axis=0)


def _sigmoid(x):
    return jax.nn.sigmoid(x)


def _softplus(x):
    return jnp.maximum(x, 0.0) + jnp.log1p(jnp.exp(-jnp.abs(x)))


def _rmsnorm(x, g):
    return x * lax.rsqrt(jnp.mean(x * x, axis=-1, keepdims=True) + NORM_EPS) * g


def _proj_kernel(h_ref, g_ref, w_ref, o_ref):
    xn = _rmsnorm(h_ref[...], g_ref[...])
    o_ref[...] = jnp.dot(xn.astype(BF16), w_ref[...], preferred_element_type=F32)


def _proj(h2d, g, w_bf16):
    rows, d = h2d.shape
    n = w_bf16.shape[1]
    return pl.pallas_call(
        _proj_kernel,
        grid=(rows // ROW_TILE,),
        in_specs=[pl.BlockSpec((ROW_TILE, d), lambda i: (i, 0)),
                  pl.BlockSpec((1, d), lambda i: (0, 0)),
                  pl.BlockSpec((d, n), lambda i: (0, 0))],
        out_specs=pl.BlockSpec((ROW_TILE, n), lambda i: (i, 0)),
        out_shape=jax.ShapeDtypeStruct((rows, n), F32),
        compiler_params=pltpu.CompilerParams(dimension_semantics=("arbitrary",),
                                             vmem_limit_bytes=V7X_VMEM_LIMIT),
        name="proj",
    )(h2d, g, w_bf16)


def _post_kernel(final, f_chunk, h_ref, y_ref, wo_ref, g_ref, wu_ref, wd_ref, gf_ref, o_ref):
    h1 = h_ref[...] + jnp.dot(y_ref[...].astype(BF16), wo_ref[...], preferred_element_type=F32)
    hb = _rmsnorm(h1, g_ref[...]).astype(BF16)
    acc = h1
    d_ff = wu_ref.shape[1]
    for c in range(d_ff // f_chunk):
        u = jnp.dot(hb, wu_ref[:, c * f_chunk:(c + 1) * f_chunk], preferred_element_type=F32)
        a = jnp.square(jnp.maximum(u, 0.0)).astype(BF16)
        acc = acc + jnp.dot(a, wd_ref[c * f_chunk:(c + 1) * f_chunk, :],
                            preferred_element_type=F32)
    if final:
        acc = _rmsnorm(acc, gf_ref[...])
    o_ref[...] = acc


def _post(h2d, y2d, wo, g, wu, wd, gf, final):
    rows, d = h2d.shape
    d_ff = wu.shape[1]
    tile = ROW_TILE // 2
    const = lambda i: (0, 0)
    return pl.pallas_call(
        functools.partial(_post_kernel, final, 1024),
        grid=(rows // tile,),
        in_specs=[pl.BlockSpec((tile, d), lambda i: (i, 0)),
                  pl.BlockSpec((tile, D_MIX), lambda i: (i, 0)),
                  pl.BlockSpec((D_MIX, d), const),
                  pl.BlockSpec((1, d), const),
                  pl.BlockSpec((d, d_ff), const),
                  pl.BlockSpec((d_ff, d), const),
                  pl.BlockSpec((1, d), const)],
        out_specs=pl.BlockSpec((tile, d), lambda i: (i, 0)),
        out_shape=jax.ShapeDtypeStruct((rows, d), F32),
        compiler_params=pltpu.CompilerParams(dimension_semantics=("arbitrary",),
                                             vmem_limit_bytes=V7X_VMEM_LIMIT),
        name="post",
    )(h2d, y2d, wo, g, wu, wd, gf)


def _rows(j):
    return slice(j * CHUNK, (j + 1) * CHUNK)


def _last_rows(x):
    return jnp.concatenate([jnp.broadcast_to(x[(j + 1) * CHUNK - 1:(j + 1) * CHUNK, :], (CHUNK, x.shape[1]))
                            for j in range(SUB)], axis=0)


def _group_a(p_ref, cw_ref, cb_ref, wri_ref, bri_ref, lam_ref, xbuf, h_a):
    t = STEP
    xa = p_ref[0, :, A_OFF:A_OFF + A_WIDTH]
    ga = p_ref[0, :, A_OFF + A_WIDTH:A_OFF + 2 * A_WIDTH]
    xbuf[8:8 + t, :] = xa
    u = cb_ref[...]
    for j in range(4):
        u = u + cw_ref[j:j + 1, :] * xbuf[5 + j:5 + j + t, :]
    xbuf[0:8, :] = xbuf[t:t + 8, :]
    ri = _dot(u, wri_ref[...]) + bri_ref[...]
    r = _sigmoid(ri[:, :A_WIDTH])
    i = _sigmoid(ri[:, A_WIDTH:])
    log_a = (-A_C * r) * _softplus(-lam_ref[...])
    a = jnp.exp(log_a)
    d = jnp.sqrt(1.0 - a * a) * (i * u)
    rows = lax.broadcasted_iota(jnp.int32, (t, A_WIDTH), 0)
    s = 1
    while s < t:
        a_sh = jnp.where(rows < s, 1.0, pltpu.roll(a, s, 0))
        d_sh = jnp.where(rows < s, 0.0, pltpu.roll(d, s, 0))
        d = a * d_sh + d
        a = a * a_sh
        s *= 2
    hseq = d + a * h_a[0:1, :]
    h_a[0:1, :] = hseq[t - 1:t, :]
    return hseq * jax.nn.gelu(ga, approximate=True)


def _group_c(p_ref, lb_ref, cn_ref, tril, wlev, bo, bdm, s_c):
    t = CHUNK
    q_raw = p_ref[0, :, C_OFF:C_OFF + C_WIDTH]
    fz = p_ref[0, :, C_OFF + C_WIDTH:C_OFF + 2 * C_WIDTH]
    v = p_ref[0, :, C_OFF + 2 * C_WIDTH:C_OFF + 3 * C_WIDTH]
    og = p_ref[0, :, C_OFF + 3 * C_WIDTH:C_OFF + 4 * C_WIDTH]
    lb = lb_ref[...]
    f = lb + (1.0 - lb) * _sigmoid(fz)
    g = jnp.log(f)
    kd = 1.0 - f
    q = q_raw * _sigmoid(q_raw)
    b = _sel_left(tril, g)
    e_lev = jnp.exp(_sel_left(wlev, g))
    b_last = _last_rows(b)
    qe = q * jnp.exp(b)
    kl = kd * jnp.exp(b_last - b)
    g_last = jnp.exp(b_last)
    dq = _dot(q * kd, bo)
    ti = lax.broadcasted_iota(jnp.int32, (t, PAIR), 0)
    lane = lax.broadcasted_iota(jnp.int32, (t, PAIR), 1)
    first = lane < C_HEAD
    sj = lane & (C_HEAD - 1)
    xs = jnp.where(ti > sj, ti ^ sj, 0)
    npair = C_WIDTH // PAIR
    cs = [(j, p) for j in range(SUB) for p in range(npair)]
    sl = lambda x, c: x[_rows(c[0]), c[1] * PAIR:(c[1] + 1) * PAIR]
    att = [jnp.zeros((t, PAIR), F32) for _ in cs]
    for l in range(C_LEVELS):
        el = e_lev[l * STEP:(l + 1) * STEP, :]
        ql = q * el
        kl_l = kd * el
        lev = (xs >> l) == 1
        att = [att[i] + jnp.where(lev, _dot(sl(ql, c), _bd(sl(kl_l, c), first), 1, 1), 0.0)
               for i, c in enumerate(cs)]
    intra = [_dot(att[i], _bd(sl(v, c), first)) + sl(dq, c) * sl(v, c) for i, c in enumerate(cs)]
    upd = [jnp.where(bdm, _dot(sl(v, c), sl(kl, c), 0, 0), 0.0) for c in cs]
    sts = {}
    for p in range(npair):
        st = s_c[p]
        for j in range(SUB):
            i = cs.index((j, p))
            sts[i] = st
            st = st * g_last[(j + 1) * t - 1:(j + 1) * t, p * PAIR:(p + 1) * PAIR] + upd[i]
        s_c[p] = st
    outs = [intra[i] + _dot(sl(qe, c), sts[i], 1, 1) for i, c in enumerate(cs)]
    o = jnp.concatenate([jnp.concatenate([outs[cs.index((j, p))] for p in range(npair)], axis=1)
                         for j in range(SUB)], axis=0)
    ms = _dot(o * o, bo) * (1.0 / C_HEAD)
    return o * lax.rsqrt(ms + NORM_EPS) * cn_ref[...] * (og * _sigmoid(og))


def _group_b(p_ref, vf_ref, vfo_ref, mu_ref, wl_ref, w0_ref, a0_ref, kk_ref, ka_ref, rk_ref,
             lng_ref, lnb_ref, v0_ref, v1_ref, v2_ref, tril, bo, bdm, pbuf, s_b):
    t = CHUNK
    w = B_WIDTH
    pb = p_ref[0, :, B_OFF:B_OFF + B_COLS]
    pbuf[8:8 + STEP, :] = pb
    prev = pbuf[7:7 + STEP, :]
    pbuf[0:8, :] = pbuf[STEP:STEP + 8, :]
    sh = pb + mu_ref[...] * (prev - pb)
    r = sh[:, 0:w]
    k = sh[:, w:2 * w]
    v = sh[:, 2 * w:3 * w]
    lo = sh[:, 3 * w:3 * w + B_LORA]
    lane = lax.broadcasted_iota(jnp.int32, (STEP, B_LORA), 1)
    xl = jnp.where(lane < 32, jnp.tanh(lo), jnp.where(lane < 64, lo, _sigmoid(lo)))
    lora = _dot(xl, wl_ref[...])
    w_log = -_softplus(-(w0_ref[...] + lora[:, 0:w])) - 0.5
    lw = -jnp.exp(w_log)
    if vf_ref is None:
        vfo_ref[0] = v
    else:
        mix = _sigmoid(v0_ref[...] + _dot(_dot(v, v1_ref[...]), v2_ref[...]))
        v = v + (vf_ref[0] - v) * mix
    aa = _sigmoid(a0_ref[...] + lora[:, w:2 * w])
    gate = lora[:, 2 * w:3 * w]
    kk = k * kk_ref[...]
    kk = kk / jnp.maximum(jnp.sqrt(_dot(kk * kk, bo)), 1e-12)
    k = k * (1.0 + (aa - 1.0) * ka_ref[...])
    ap = -kk
    bb = kk * aa
    c = _sel_left(tril, lw)
    c_last = _last_rows(c)
    a_h = ap * jnp.exp(c - lw)
    r_h = r * jnp.exp(c)
    e_inv = jnp.exp(-c)
    b_h = bb * e_inv
    k_h = k * e_inv
    e_cl = jnp.exp(c_last - c)
    b_c = bb * e_cl
    k_c = k * e_cl
    gam = jnp.exp(c_last)

    ti = lax.broadcasted_iota(jnp.int32, (t, PAIR), 0)
    lane = lax.broadcasted_iota(jnp.int32, (t, PAIR), 1)
    first = lane < B_HEAD
    sj = lane & (B_HEAD - 1)
    strict = ti > sj
    lower = ti >= sj
    blk = (ti >> 4) == (sj >> 4)
    eye = jnp.where(ti == sj, 1.0, 0.0)
    bd = lambda y: _bd(y, first)
    cat0 = lambda xs: jnp.concatenate(xs, axis=0)
    cat1 = lambda xs: jnp.concatenate(xs, axis=1)
    npair = B_WIDTH // PAIR
    cs = [(j, p) for j in range(SUB) for p in range(npair)]
    n = range(len(cs))
    sl = lambda x, i: x[_rows(cs[i][0]), cs[i][1] * PAIR:(cs[i][1] + 1) * PAIR]
    vbd = [bd(sl(v, i)) for i in n]
    ar = [cat0([sl(a_h, i), sl(r_h, i)]) for i in n]
    mm = [_dot(ar[i], cat0([bd(sl(b_h, i)), bd(sl(k_h, i))]), 1, 1) for i in n]
    m_ab = [jnp.where(strict, m[:t, :PAIR], 0.0) for m in mm]
    m_ak = [jnp.where(strict, m[:t, PAIR:], 0.0) for m in mm]
    n_rb = [jnp.where(lower, m[t:, :PAIR], 0.0) for m in mm]
    n_rk = [jnp.where(lower, m[t:, PAIR:], 0.0) for m in mm]
    dg = [jnp.where(blk, m, 0.0) for m in m_ab]
    off = [m_ab[i] - dg[i] for i in n]
    akv = [_dot(m_ak[i], vbd[i]) for i in n]
    nkv = [_dot(n_rk[i], vbd[i]) for i in n]
    d2 = [_dot(dg[i], bd(dg[i])) for i in n]
    s4 = [_dot(cat0([d2[i], dg[i]]), bd(d2[i])) for i in n]
    d4 = [x[:t] for x in s4]
    x1 = [eye + dg[i] + d2[i] + s4[i][t:] for i in n]
    s8 = [_dot(cat0([d4[i], x1[i]]), bd(d4[i])) for i in n]
    x2 = [x1[i] + s8[i][t:] for i in n]
    dinv = [x2[i] + _dot(x2[i], bd(s8[i][:t])) for i in n]
    gm = [_dot(dinv[i], bd(off[i])) for i in n]
    g2 = [_dot(gm[i], bd(gm[i])) for i in n]
    gs = [eye + gm[i] + g2[i] + _dot(gm[i], bd(g2[i])) for i in n]
    tinv = [_dot(gs[i], bd(dinv[i])) for i in n]
    pq = [_dot(tinv[i], cat1([bd(sl(a_h, i)), bd(akv[i])])) for i in n]
    npq = [_dot(n_rb[i], cat1([bd(pq[i][:, :PAIR]), bd(pq[i][:, PAIR:])])) for i in n]
    p2 = [sl(r_h, i) + npq[i][:, :PAIR] for i in n]
    q2 = [npq[i][:, PAIR:] + nkv[i] for i in n]
    pqb = [_dot(pq[i], sl(b_c, i), 0, 0) for i in n]
    vk = [_dot(sl(v, i), sl(k_c, i), 0, 0) for i in n]
    wp = [jnp.where(bdm, x[:PAIR], 0.0) for x in pqb]
    zz = [jnp.where(bdm, pqb[i][PAIR:] + vk[i], 0.0) for i in n]
    sts = {}
    for p in range(npair):
        st = s_b[p]
        for j in range(SUB):
            i = cs.index((j, p))
            sts[i] = st
            st = st * gam[(j + 1) * t - 1:(j + 1) * t, p * PAIR:(p + 1) * PAIR] + _dot(st, wp[i]) + zz[i]
        s_b[p] = st
    ys = [_dot(p2[i], sts[i], 1, 1) + q2[i] for i in n]
    y = cat0([cat1([ys[cs.index((j, p))] for p in range(npair)]) for j in range(SUB)])
    inv_n = 1.0 / B_HEAD
    mean = _dot(y, bo) * inv_n
    yc = y - mean
    var = _dot(yc * yc, bo) * inv_n
    y = yc * lax.rsqrt(var + B_GN_EPS) * lng_ref[...] + lnb_ref[...]
    y = y + _dot(r * k * rk_ref[...], bo) * v
    return y * gate


def _mixer_kernel(layer0, *refs):
    it = iter(refs)
    p_ref = next(it)
    vf_ref = None if layer0 else next(it)
    cw, cb, wri, bri, lam = (next(it) for _ in range(5))
    mu, wl, w0, a0, kkp, kap, rkp, lng, lnb = (next(it) for _ in range(9))
    v0 = v1 = v2 = None
    if not layer0:
        v0, v1, v2 = (next(it) for _ in range(3))
    lb, cn, tril_ref, wlev_ref, bo_ref = (next(it) for _ in range(5))
    y_ref = next(it)
    vfo_ref = next(it) if layer0 else None
    xbuf, h_a, pbuf, s_b, s_c = (next(it) for _ in range(5))

    @pl.when(pl.program_id(1) == 0)
    def _():
        xbuf[0:8, :] = jnp.zeros((8, A_WIDTH), F32)
        pbuf[0:8, :] = jnp.zeros((8, B_COLS), F32)
        h_a[...] = jnp.zeros_like(h_a)
        s_b[...] = jnp.zeros_like(s_b)
        s_c[...] = jnp.zeros_like(s_c)

    tril = tril_ref[...]
    bo = bo_ref[...]
    half = lambda d: lax.broadcasted_iota(jnp.int32, (PAIR, PAIR), d) < PAIR // 2
    bdm = half(0) == half(1)
    y_ref[0, :, A_WIDTH:A_WIDTH + B_WIDTH] = _group_b(
        p_ref, vf_ref, vfo_ref, mu, wl, w0, a0, kkp, kap, rkp, lng, lnb, v0, v1, v2,
        tril, bo, bdm, pbuf, s_b)
    y_ref[0, :, A_WIDTH + B_WIDTH:D_MIX] = _group_c(
        p_ref, lb, cn, tril, wlev_ref[...], bo[:C_WIDTH, :C_WIDTH], bdm, s_c)
    y_ref[0, :, 0:A_WIDTH] = _group_a(p_ref, cw, cb, wri, bri, lam, xbuf, h_a)


def _level_matrix():
    t = CHUNK
    w = np.zeros((C_LEVELS, t, t), np.float32)
    for l in range(C_LEVELS):
        half = 1 << l
        for row in range(t):
            base = (row >> (l + 1)) << (l + 1)
            if (row >> l) & 1:
                w[l, row, base + half:row + 1] = 1.0
            else:
                w[l, row, row + 1:base + half] = 1.0
    return w


def _step_constants():
    eye = np.eye(SUB, dtype=np.float32)
    tril = np.kron(eye, np.tril(np.ones((CHUNK, CHUNK), np.float32)))
    wlev = np.concatenate([np.kron(eye, lv) for lv in _level_matrix()], axis=0)
    hid = np.arange(B_WIDTH) // B_HEAD
    bo = (hid[:, None] == hid[None, :]).astype(np.float32)
    return tuple(jnp.asarray(m, BF16) for m in (tril, wlev, bo))


def _mixer(proj, v_first, params, consts, layer0):
    bsz, lp, _ = proj.shape
    t = STEP
    const = lambda b, c: (0, 0)
    row = lambda b, c: (b, c, 0)
    ins = [proj]
    specs = [pl.BlockSpec((1, t, N_IN), row)]
    if not layer0:
        ins.append(v_first)
        specs.append(pl.BlockSpec((1, t, B_WIDTH), row))
    for arr in list(params) + list(consts):
        ins.append(arr)
        specs.append(pl.BlockSpec(arr.shape, const))
    out_shape = [jax.ShapeDtypeStruct((bsz, lp, D_MIX), F32)]
    out_specs = [pl.BlockSpec((1, t, D_MIX), row)]
    if layer0:
        out_shape.append(jax.ShapeDtypeStruct((bsz, lp, B_WIDTH), F32))
        out_specs.append(pl.BlockSpec((1, t, B_WIDTH), row))
    res = pl.pallas_call(
        functools.partial(_mixer_kernel, layer0),
        grid=(bsz, lp // t),
        in_specs=specs,
        out_specs=out_specs,
        out_shape=out_shape,
        scratch_shapes=[pltpu.VMEM((t + 8, A_WIDTH), F32),
                        pltpu.VMEM((8, A_WIDTH), F32),
                        pltpu.VMEM((t + 8, B_COLS), F32),
                        pltpu.VMEM((B_WIDTH // PAIR, PAIR, PAIR), F32),
                        pltpu.VMEM((C_WIDTH // PAIR, PAIR, PAIR), F32)],
        compiler_params=pltpu.CompilerParams(dimension_semantics=("arbitrary", "arbitrary"),
                                             vmem_limit_bytes=V7X_VMEM_LIMIT),
        name="mixer0" if layer0 else "mixer",
    )(*ins)
    return res if layer0 else (res[0], v_first)


def _block_diag(w):
    hn, n, _ = w.shape
    out = jnp.zeros((hn * n, hn * n), w.dtype)
    for h in range(hn):
        out = out.at[h * n:(h + 1) * n, h * n:(h + 1) * n].set(w[h])
    return out


def kernel(x, meta, mix_norm, w_in, a_conv_w, a_conv_b, a_w_r, a_b_r, a_w_i, a_b_i, a_lambda, b_mu, b_w0, b_w2, b_a0, b_a2, b_g2, b_k_k, b_k_a, b_r_k, b_ln_g, b_ln_b, b_v0, b_v1, b_v2, c_lb, c_norm, w_out, ffn_norm, w_up, w_down, final_norm):
    bsz, seq, d = x.shape
    depth = w_in.shape[0]
    l_real = N_META + seq
    lp = -(-l_real // STEP) * STEP
    while (bsz * lp) % ROW_TILE:
        lp += STEP
    h = jnp.concatenate([jnp.broadcast_to(meta[None].astype(x.dtype), (bsz, N_META, d)), x,
                         jnp.zeros((bsz, lp - l_real, d), x.dtype)], axis=1)
    h = h.reshape(bsz * lp, d)

    lb_all = jnp.cumsum(jax.nn.softmax(c_lb.astype(F32), axis=0), axis=0)
    lb_all = lb_all - lb_all[0]
    consts = _step_constants()
    row = lambda a: a.reshape(1, -1)

    v_first = None
    for l in range(depth):
        proj = _proj(h, row(mix_norm[l]), w_in[l].astype(BF16)).reshape(bsz, lp, N_IN)
        wri = jnp.concatenate([_block_diag(a_w_r[l]), _block_diag(a_w_i[l])], axis=1).astype(BF16)
        bri = jnp.concatenate([a_b_r[l], a_b_i[l]]).reshape(1, -1)
        wl = jnp.zeros((B_LORA, 3 * B_WIDTH), F32)
        wl = wl.at[0:32, 0:B_WIDTH].set(b_w2[l])
        wl = wl.at[32:64, B_WIDTH:2 * B_WIDTH].set(b_a2[l])
        wl = wl.at[64:128, 2 * B_WIDTH:].set(b_g2[l])
        params = [a_conv_w[l], row(a_conv_b[l]), wri, bri, row(a_lambda[l]),
                  row(b_mu[l]), wl.astype(BF16), row(b_w0[l]), row(b_a0[l]), row(b_k_k[l]),
                  row(b_k_a[l]), row(b_r_k[l]), row(b_ln_g[l]), row(b_ln_b[l])]
        if l > 0:
            params += [row(b_v0[l - 1]), b_v1[l - 1].astype(BF16), b_v2[l - 1].astype(BF16)]
        params += [row(lb_all[l]), row(jnp.tile(c_norm[l], C_HEADS))]
        y, v_first = _mixer(proj, v_first, params, consts, l == 0)
        h = _post(h, y.reshape(bsz * lp, D_MIX), w_out[l].astype(BF16), row(ffn_norm[l]),
                  w_up[l].astype(BF16), w_down[l].astype(BF16), row(final_norm), l == depth - 1)
    return h.reshape(bsz, lp, d)[:, N_META:l_real]
```

```python
import functools

import numpy as np
import jax
import jax.numpy as jnp
from jax import lax
from jax.experimental import pallas as pl
from jax.experimental.pallas import tpu as pltpu

F32 = jnp.float32
BF16 = jnp.bfloat16

N_META = 16
NORM_EPS = 1e-6
CHUNK = 64
SUB = 3
STEP = SUB * CHUNK

A_WIDTH = 384
A_HEADS = 6
A_C = 8.0
B_WIDTH = 384
B_HEAD = 64
B_HEADS = 6
B_LORA = 128
B_COLS = 3 * B_WIDTH + B_LORA
B_GN_EPS = 64e-5
C_WIDTH = 256
C_HEADS = 4
C_HEAD = 64
C_LEVELS = 6
PAIR = 128
A_OFF = 0
B_OFF = 2 * A_WIDTH
C_OFF = B_OFF + B_COLS
N_IN = C_OFF + 4 * C_WIDTH
D_MIX = A_WIDTH + B_WIDTH + C_WIDTH

V7X_VMEM_LIMIT = 56 * 1024 * 1024
ROW_TILE = 512


def _dot(a, b, ca=1, cb=0):
    dims = (((ca,), (cb,)), ((), ()))
    return lax.dot_general(a.astype(BF16), b.astype(BF16), dims, preferred_element_type=F32)


def _split2(x):
    hi = x.astype(BF16)
    return hi, (x - hi.astype(F32)).astype(BF16)


def _sel_left(m, x):
    n = x.shape[1]
    r = jnp.dot(m, jnp.concatenate(_split2(x), axis=1), preferred_element_type=F32)
    return r[:, :n] + r[:, n:]


def _bd(y, first):
    return jnp.concatenate([jnp.where(first, y, 0.0), jnp.where(first, 0.0, y)], axis=0)


def _sigmoid(x):
    return jax.nn.sigmoid(x)


def _softplus(x):
    return jnp.maximum(x, 0.0) + jnp.log1p(jnp.exp(-jnp.abs(x)))


def _rmsnorm(x, g):
    return x * lax.rsqrt(jnp.mean(x * x, axis=-1, keepdims=True) + NORM_EPS) * g


def _proj_kernel(h_ref, g_ref, w_ref, o_ref):
    xn = _rmsnorm(h_ref[...], g_ref[...])
    o_ref[...] = jnp.dot(xn.astype(BF16), w_ref[...], preferred_element_type=F32)


def _proj(h2d, g, w_bf16):
    rows, d = h2d.shape
    n = w_bf16.shape[1]
    return pl.pallas_call(
        _proj_kernel,
        grid=(rows // ROW_TILE,),
        in_specs=[pl.BlockSpec((ROW_TILE, d), lambda i: (i, 0)),
                  pl.BlockSpec((1, d), lambda i: (0, 0)),
                  pl.BlockSpec((d, n), lambda i: (0, 0))],
        out_specs=pl.BlockSpec((ROW_TILE, n), lambda i: (i, 0)),
        out_shape=jax.ShapeDtypeStruct((rows, n), F32),
        compiler_params=pltpu.CompilerParams(dimension_semantics=("arbitrary",),
                                             vmem_limit_bytes=V7X_VMEM_LIMIT),
        name="proj",
    )(h2d, g, w_bf16)


def _post_kernel(final, f_chunk, h_ref, y_ref, wo_ref, g_ref, wu_ref, wd_ref, gf_ref, o_ref):
    h1 = h_ref[...] + jnp.dot(y_ref[...].astype(BF16), wo_ref[...], preferred_element_type=F32)
    hb = _rmsnorm(h1, g_ref[...]).astype(BF16)
    acc = h1
    d_ff = wu_ref.shape[1]
    for c in range(d_ff // f_chunk):
        u = jnp.dot(hb, wu_ref[:, c * f_chunk:(c + 1) * f_chunk], preferred_element_type=F32)
        a = jnp.square(jnp.maximum(u, 0.0)).astype(BF16)
        acc = acc + jnp.dot(a, wd_ref[c * f_chunk:(c + 1) * f_chunk, :],
                            preferred_element_type=F32)
    if final:
        acc = _rmsnorm(acc, gf_ref[...])
    o_ref[...] = acc


def _post(h2d, y2d, wo, g, wu, wd, gf, final):
    rows, d = h2d.shape
    d_ff = wu.shape[1]
    tile = ROW_TILE // 2
    const = lambda i: (0, 0)
    return pl.pallas_call(
        functools.partial(_post_kernel, final, 1024),
        grid=(rows // tile,),
        in_specs=[pl.BlockSpec((tile, d), lambda i: (i, 0)),
                  pl.BlockSpec((tile, D_MIX), lambda i: (i, 0)),
                  pl.BlockSpec((D_MIX, d), const),
                  pl.BlockSpec((1, d), const),
                  pl.BlockSpec((d, d_ff), const),
                  pl.BlockSpec((d_ff, d), const),
                  pl.BlockSpec((1, d), const)],
        out_specs=pl.BlockSpec((tile, d), lambda i: (i, 0)),
        out_shape=jax.ShapeDtypeStruct((rows, d), F32),
        compiler_params=pltpu.CompilerParams(dimension_semantics=("arbitrary",),
                                             vmem_limit_bytes=V7X_VMEM_LIMIT),
        name="post",
    )(h2d, y2d, wo, g, wu, wd, gf)


def _rows(j):
    return slice(j * CHUNK, (j + 1) * CHUNK)


def _last_rows(x):
    return jnp.concatenate([jnp.broadcast_to(x[(j + 1) * CHUNK - 1:(j + 1) * CHUNK, :], (CHUNK, x.shape[1]))
                            for j in range(SUB)], axis=0)


def _group_a(p_ref, o_ref, cw_ref, cb_ref, wri_ref, bri_ref, lam_ref, xbuf, h_a):
    t = STEP
    xa = p_ref[0, :, A_OFF:A_OFF + A_WIDTH]
    ga = p_ref[0, :, A_OFF + A_WIDTH:A_OFF + 2 * A_WIDTH]
    xbuf[8:8 + t, :] = xa
    u = cb_ref[...]
    for j in range(4):
        u = u + cw_ref[j:j + 1, :] * xbuf[5 + j:5 + j + t, :]
    xbuf[0:8, :] = xbuf[t:t + 8, :]
    yield
    ri = _dot(u, wri_ref[...]) + bri_ref[...]
    r = _sigmoid(ri[:, :A_WIDTH])
    i = _sigmoid(ri[:, A_WIDTH:])
    yield
    log_a = (-A_C * r) * _softplus(-lam_ref[...])
    a = jnp.exp(log_a)
    om = 1.0 - a * a
    d = (om * lax.rsqrt(jnp.maximum(om, 1e-30))) * (i * u)
    yield
    rows = lax.broadcasted_iota(jnp.int32, (t, A_WIDTH), 0) & 7
    for s in (1, 2, 4):
        a_sh = jnp.where(rows < s, 1.0, pltpu.roll(a, s, 0))
        d_sh = jnp.where(rows < s, 0.0, pltpu.roll(d, s, 0))
        d = a * d_sh + d
        a = a * a_sh
        yield
    carry = h_a[0:1, :]
    hs = []
    for g in range(t // 8):
        hg = d[8 * g:8 * g + 8, :] + a[8 * g:8 * g + 8, :] * carry
        carry = hg[7:8, :]
        hs.append(hg)
        if g % 8 == 7:
            yield
    hseq = jnp.concatenate(hs, axis=0)
    h_a[0:1, :] = carry
    o_ref[0, :, 0:A_WIDTH] = hseq * jax.nn.gelu(ga, approximate=True)


def _group_c(p_ref, o_ref, lb_ref, cn_ref, tril, wlev, bo, bdm, s_c):
    t = CHUNK
    q_raw = p_ref[0, :, C_OFF:C_OFF + C_WIDTH]
    fz = p_ref[0, :, C_OFF + C_WIDTH:C_OFF + 2 * C_WIDTH]
    v = p_ref[0, :, C_OFF + 2 * C_WIDTH:C_OFF + 3 * C_WIDTH]
    og = p_ref[0, :, C_OFF + 3 * C_WIDTH:C_OFF + 4 * C_WIDTH]
    lb = lb_ref[...]
    f = lb + (1.0 - lb) * _sigmoid(fz)
    g = jnp.log(f)
    kd = 1.0 - f
    q = q_raw * _sigmoid(q_raw)
    yield
    b = _sel_left(tril, g)
    e_lev = jnp.exp(_sel_left(wlev, g))
    b_last = _last_rows(b)
    qe = q * jnp.exp(b)
    kl = kd * jnp.exp(b_last - b)
    g_last = jnp.exp(b_last)
    dq = _dot(q * kd, bo)
    yield
    ti = lax.broadcasted_iota(jnp.int32, (t, PAIR), 0)
    lane = lax.broadcasted_iota(jnp.int32, (t, PAIR), 1)
    first = lane < C_HEAD
    sj = lane & (C_HEAD - 1)
    xs = jnp.where(ti > sj, ti ^ sj, 0)
    npair = C_WIDTH // PAIR
    cs = [(j, p) for j in range(SUB) for p in range(npair)]
    sl = lambda x, c: x[_rows(c[0]), c[1] * PAIR:(c[1] + 1) * PAIR]
    att = [jnp.zeros((t, PAIR), F32) for _ in cs]
    for l in range(C_LEVELS):
        el = e_lev[l * STEP:(l + 1) * STEP, :]
        ql = q * el
        kl_l = kd * el
        lev = (xs >> l) == 1
        att = [att[i] + jnp.where(lev, _dot(sl(ql, c), _bd(sl(kl_l, c), first), 1, 1), 0.0)
               for i, c in enumerate(cs)]
        yield
    intra = [_dot(att[i], _bd(sl(v, c), first)) + sl(dq, c) * sl(v, c) for i, c in enumerate(cs)]
    upd = [jnp.where(bdm, _dot(sl(v, c), sl(kl, c), 0, 0), 0.0) for c in cs]
    yield
    sts = {}
    for p in range(npair):
        st = s_c[p]
        for j in range(SUB):
            i = cs.index((j, p))
            sts[i] = st
            st = st * g_last[(j + 1) * t - 1:(j + 1) * t, p * PAIR:(p + 1) * PAIR] + upd[i]
        s_c[p] = st
    outs = [intra[i] + _dot(sl(qe, c), sts[i], 1, 1) for i, c in enumerate(cs)]
    o = jnp.concatenate([jnp.concatenate([outs[cs.index((j, p))] for p in range(npair)], axis=1)
                         for j in range(SUB)], axis=0)
    yield
    ms = _dot(o * o, bo) * (1.0 / C_HEAD)
    o_ref[0, :, A_WIDTH + B_WIDTH:D_MIX] = o * lax.rsqrt(ms + NORM_EPS) * cn_ref[...] * (og * _sigmoid(og))


def _group_b(p_ref, o_ref, vf_ref, vfo_ref, mu_ref, wl_ref, w0_ref, a0_ref, kk_ref, ka_ref, rk_ref,
             lng_ref, lnb_ref, v0_ref, v1_ref, v2_ref, tril, bo, bdm, pbuf, s_b):
    t = CHUNK
    w = B_WIDTH
    pb = p_ref[0, :, B_OFF:B_OFF + B_COLS]
    pbuf[8:8 + STEP, :] = pb
    prev = pbuf[7:7 + STEP, :]
    pbuf[0:8, :] = pbuf[STEP:STEP + 8, :]
    sh = pb + mu_ref[...] * (prev - pb)
    r = sh[:, 0:w]
    k = sh[:, w:2 * w]
    v = sh[:, 2 * w:3 * w]
    lo = sh[:, 3 * w:3 * w + B_LORA]
    lane = lax.broadcasted_iota(jnp.int32, (STEP, B_LORA), 1)
    xl = jnp.where(lane < 32, jnp.tanh(lo), jnp.where(lane < 64, lo, _sigmoid(lo)))
    lora = _dot(xl, wl_ref[...])
    lw = -float(np.exp(-0.5)) * _sigmoid(w0_ref[...] + lora[:, 0:w])
    if vf_ref is None:
        vfo_ref[0] = v
    else:
        mix = _sigmoid(v0_ref[...] + _dot(_dot(v, v1_ref[...]), v2_ref[...]))
        v = v + (vf_ref[0] - v) * mix
    aa = _sigmoid(a0_ref[...] + lora[:, w:2 * w])
    gate = lora[:, 2 * w:3 * w]
    kk = k * kk_ref[...]
    kk = kk * lax.rsqrt(jnp.maximum(_dot(kk * kk, bo), 1e-24))
    k = k * (1.0 + (aa - 1.0) * ka_ref[...])
    ap = -kk
    bb = kk * aa
    c = _sel_left(tril, lw)
    c_last = _last_rows(c)
    a_h = ap * jnp.exp(c - lw)
    r_h = r * jnp.exp(c)
    e_inv = jnp.exp(-c)
    b_h = bb * e_inv
    k_h = k * e_inv
    e_cl = jnp.exp(c_last - c)
    b_c = bb * e_cl
    k_c = k * e_cl
    gam = jnp.exp(c_last)
    yield

    ti = lax.broadcasted_iota(jnp.int32, (t, PAIR), 0)
    lane = lax.broadcasted_iota(jnp.int32, (t, PAIR), 1)
    first = lane < B_HEAD
    sj = lane & (B_HEAD - 1)
    strict = ti > sj
    lower = ti >= sj
    blk = (ti >> 4) == (sj >> 4)
    eye = jnp.where(ti == sj, 1.0, 0.0)
    bd = lambda y: _bd(y, first)
    cat0 = lambda xs: jnp.concatenate(xs, axis=0)
    cat1 = lambda xs: jnp.concatenate(xs, axis=1)
    npair = B_WIDTH // PAIR
    cs = [(j, p) for j in range(SUB) for p in range(npair)]
    n = range(len(cs))
    sl = lambda x, i: x[_rows(cs[i][0]), cs[i][1] * PAIR:(cs[i][1] + 1) * PAIR]
    vbd = [bd(sl(v, i)) for i in n]
    ar = [cat0([sl(a_h, i), sl(r_h, i)]) for i in n]
    mm = [_dot(ar[i], cat0([bd(sl(b_h, i)), bd(sl(k_h, i))]), 1, 1) for i in n]
    yield
    m_ab = [jnp.where(strict, m[:t, :PAIR], 0.0) for m in mm]
    m_ak = [jnp.where(strict, m[:t, PAIR:], 0.0) for m in mm]
    n_rb = [jnp.where(lower, m[t:, :PAIR], 0.0) for m in mm]
    n_rk = [jnp.where(lower, m[t:, PAIR:], 0.0) for m in mm]
    dg = [jnp.where(blk, m, 0.0) for m in m_ab]
    off = [m_ab[i] - dg[i] for i in n]
    akv = [_dot(m_ak[i], vbd[i]) for i in n]
    nkv = [_dot(n_rk[i], vbd[i]) for i in n]
    d2 = [_dot(dg[i], bd(dg[i])) for i in n]
    yield
    s4 = [_dot(cat0([d2[i], dg[i]]), bd(d2[i])) for i in n]
    yield
    d4 = [x[:t] for x in s4]
    x1 = [eye + dg[i] + d2[i] + s4[i][t:] for i in n]
    s8 = [_dot(cat0([d4[i], x1[i]]), bd(d4[i])) for i in n]
    yield
    x2 = [x1[i] + s8[i][t:] for i in n]
    dinv = [x2[i] + _dot(x2[i], bd(s8[i][:t])) for i in n]
    yield
    gm = [_dot(dinv[i], bd(off[i])) for i in n]
    yield
    g2 = [_dot(gm[i], bd(gm[i])) for i in n]
    yield
    gs = [eye + gm[i] + g2[i] + _dot(gm[i], bd(g2[i])) for i in n]
    yield
    tinv = [_dot(gs[i], bd(dinv[i])) for i in n]
    yield
    pq = [_dot(tinv[i], cat1([bd(sl(a_h, i)), bd(akv[i])])) for i in n]
    yield
    npq = [_dot(n_rb[i], cat1([bd(pq[i][:, :PAIR]), bd(pq[i][:, PAIR:])])) for i in n]
    yield
    p2 = [sl(r_h, i) + npq[i][:, :PAIR] for i in n]
    q2 = [npq[i][:, PAIR:] + nkv[i] for i in n]
    pqb = [_dot(pq[i], sl(b_c, i), 0, 0) for i in n]
    vk = [_dot(sl(v, i), sl(k_c, i), 0, 0) for i in n]
    yield
    wp = [jnp.where(bdm, x[:PAIR], 0.0) for x in pqb]
    zz = [jnp.where(bdm, pqb[i][PAIR:] + vk[i], 0.0) for i in n]
    sts = {}
    for p in range(npair):
        st = s_b[p]
        for j in range(SUB):
            i = cs.index((j, p))
            sts[i] = st
            st = st * gam[(j + 1) * t - 1:(j + 1) * t, p * PAIR:(p + 1) * PAIR] + _dot(st, wp[i]) + zz[i]
        s_b[p] = st
    ys = [_dot(p2[i], sts[i], 1, 1) + q2[i] for i in n]
    yield
    y = cat0([cat1([ys[cs.index((j, p))] for p in range(npair)]) for j in range(SUB)])
    inv_n = 1.0 / B_HEAD
    mean = _dot(y, bo) * inv_n
    yc = y - mean
    var = _dot(yc * yc, bo) * inv_n
    y = yc * lax.rsqrt(var + B_GN_EPS) * lng_ref[...] + lnb_ref[...]
    y = y + _dot(r * k * rk_ref[...], bo) * v
    o_ref[0, :, A_WIDTH:A_WIDTH + B_WIDTH] = y * gate


def _mixer_kernel(layer0, *refs):
    it = iter(refs)
    p_ref = next(it)
    vf_ref = None if layer0 else next(it)
    cw, cb, wri, bri, lam = (next(it) for _ in range(5))
    mu, wl, w0, a0, kkp, kap, rkp, lng, lnb = (next(it) for _ in range(9))
    v0 = v1 = v2 = None
    if not layer0:
        v0, v1, v2 = (next(it) for _ in range(3))
    lb, cn, tril_ref, wlev_ref, bo_ref = (next(it) for _ in range(5))
    y_ref = next(it)
    vfo_ref = next(it) if layer0 else None
    xbuf, h_a, pbuf, s_b, s_c = (next(it) for _ in range(5))

    @pl.when(pl.program_id(1) == 0)
    def _():
        xbuf[0:8, :] = jnp.zeros((8, A_WIDTH), F32)
        pbuf[0:8, :] = jnp.zeros((8, B_COLS), F32)
        h_a[...] = jnp.zeros_like(h_a)
        s_b[...] = jnp.zeros_like(s_b)
        s_c[...] = jnp.zeros_like(s_c)

    tril = tril_ref[...]
    bo = bo_ref[...]
    half = lambda d: lax.broadcasted_iota(jnp.int32, (PAIR, PAIR), d) < PAIR // 2
    bdm = half(0) == half(1)
    gb = _group_b(p_ref, y_ref, vf_ref, vfo_ref, mu, wl, w0, a0, kkp, kap, rkp, lng, lnb, v0, v1, v2,
                  tril, bo, bdm, pbuf, s_b)
    gc = _group_c(p_ref, y_ref, lb, cn, tril, wlev_ref[...], bo[:C_WIDTH, :C_WIDTH], bdm, s_c)
    ga = _group_a(p_ref, y_ref, cw, cb, wri, bri, lam, xbuf, h_a)
    next(gb)
    fillers = [gc, ga]
    live = [gb] + fillers
    while live:
        for gen in list(live):
            try:
                next(gen)
            except StopIteration:
                live.remove(gen)


def _level_matrix():
    t = CHUNK
    w = np.zeros((C_LEVELS, t, t), np.float32)
    for l in range(C_LEVELS):
        half = 1 << l
        for row in range(t):
            base = (row >> (l + 1)) << (l + 1)
            if (row >> l) & 1:
                w[l, row, base + half:row + 1] = 1.0
            else:
                w[l, row, row + 1:base + half] = 1.0
    return w


def _step_constants():
    eye = np.eye(SUB, dtype=np.float32)
    tril = np.kron(eye, np.tril(np.ones((CHUNK, CHUNK), np.float32)))
    wlev = np.concatenate([np.kron(eye, lv) for lv in _level_matrix()], axis=0)
    hid = np.arange(B_WIDTH) // B_HEAD
    bo = (hid[:, None] == hid[None, :]).astype(np.float32)
    return tuple(jnp.asarray(m, BF16) for m in (tril, wlev, bo))


def _mixer(proj, v_first, params, consts, layer0):
    bsz, lp, _ = proj.shape
    t = STEP
    const = lambda b, c: (0, 0)
    row = lambda b, c: (b, c, 0)
    ins = [proj]
    specs = [pl.BlockSpec((1, t, N_IN), row)]
    if not layer0:
        ins.append(v_first)
        specs.append(pl.BlockSpec((1, t, B_WIDTH), row))
    for arr in list(params) + list(consts):
        ins.append(arr)
        specs.append(pl.BlockSpec(arr.shape, const))
    out_shape = [jax.ShapeDtypeStruct((bsz, lp, D_MIX), F32)]
    out_specs = [pl.BlockSpec((1, t, D_MIX), row)]
    if layer0:
        out_shape.append(jax.ShapeDtypeStruct((bsz, lp, B_WIDTH), F32))
        out_specs.append(pl.BlockSpec((1, t, B_WIDTH), row))
    res = pl.pallas_call(
        functools.partial(_mixer_kernel, layer0),
        grid=(bsz, lp // t),
        in_specs=specs,
        out_specs=out_specs,
        out_shape=out_shape,
        scratch_shapes=[pltpu.VMEM((t + 8, A_WIDTH), F32),
                        pltpu.VMEM((8, A_WIDTH), F32),
                        pltpu.VMEM((t + 8, B_COLS), F32),
                        pltpu.VMEM((B_WIDTH // PAIR, PAIR, PAIR), F32),
                        pltpu.VMEM((C_WIDTH // PAIR, PAIR, PAIR), F32)],
        compiler_params=pltpu.CompilerParams(dimension_semantics=("arbitrary", "arbitrary"),
                                             vmem_limit_bytes=V7X_VMEM_LIMIT),
        name="mixer0" if layer0 else "mixer",
    )(*ins)
    return res if layer0 else (res[0], v_first)


def _block_diag(w):
    hn, n, _ = w.shape
    out = jnp.zeros((hn * n, hn * n), w.dtype)
    for h in range(hn):
        out = out.at[h * n:(h + 1) * n, h * n:(h + 1) * n].set(w[h])
    return out


def kernel(x, meta, mix_norm, w_in, a_conv_w, a_conv_b, a_w_r, a_b_r, a_w_i, a_b_i, a_lambda, b_mu, b_w0, b_w2, b_a0, b_a2, b_g2, b_k_k, b_k_a, b_r_k, b_ln_g, b_ln_b, b_v0, b_v1, b_v2, c_lb, c_norm, w_out, ffn_norm, w_up, w_down, final_norm):
    bsz, seq, d = x.shape
    depth = w_in.shape[0]
    l_real = N_META + seq
    lp = -(-l_real // STEP) * STEP
    while (bsz * lp) % ROW_TILE:
        lp += STEP
    h = jnp.concatenate([jnp.broadcast_to(meta[None].astype(x.dtype), (bsz, N_META, d)), x,
                         jnp.zeros((bsz, lp - l_real, d), x.dtype)], axis=1)
    h = h.reshape(bsz * lp, d)

    lb_all = jnp.cumsum(jax.nn.softmax(c_lb.astype(F32), axis=0), axis=0)
    lb_all = lb_all - lb_all[0]
    consts = _step_constants()
    row = lambda a: a.reshape(1, -1)

    v_first = None
    for l in range(depth):
        proj = _proj(h, row(mix_norm[l]), w_in[l].astype(BF16)).reshape(bsz, lp, N_IN)
        wri = jnp.concatenate([_block_diag(a_w_r[l]), _block_diag(a_w_i[l])], axis=1).astype(BF16)
        bri = jnp.concatenate([a_b_r[l], a_b_i[l]]).reshape(1, -1)
        wl = jnp.zeros((B_LORA, 3 * B_WIDTH), F32)
        wl = wl.at[0:32, 0:B_WIDTH].set(b_w2[l])
        wl = wl.at[32:64, B_WIDTH:2 * B_WIDTH].set(b_a2[l])
        wl = wl.at[64:128, 2 * B_WIDTH:].set(b_g2[l])
        params = [a_conv_w[l], row(a_conv_b[l]), wri, bri, row(a_lambda[l]),
                  row(b_mu[l]), wl.astype(BF16), row(b_w0[l]), row(b_a0[l]), row(b_k_k[l]),
                  row(b_k_a[l]), row(b_r_k[l]), row(b_ln_g[l]), row(b_ln_b[l])]
        if l > 0:
            params += [row(b_v0[l - 1]), b_v1[l - 1].astype(BF16), b_v2[l - 1].astype(BF16)]
        params += [row(lb_all[l]), row(jnp.tile(c_norm[l], C_HEADS))]
        y, v_first = _mixer(proj, v_first, params, consts, l == 0)
        h = _post(h, y.reshape(bsz * lp, D_MIX), w_out[l].astype(BF16), row(ffn_norm[l]),
                  w_up[l].astype(BF16), w_down[l].astype(BF16), row(final_norm), l == depth - 1)
    return h.reshape(bsz, lp, d)[:, N_META:l_real]
```

```python
import functools

import numpy as np
import jax
import jax.numpy as jnp
from jax import lax
from jax.experimental import pallas as pl
from jax.experimental.pallas import tpu as pltpu

F32 = jnp.float32
BF16 = jnp.bfloat16

N_META = 16
NORM_EPS = 1e-6
CHUNK = 64
SUB = 3
NBATCH = 2
STEP = SUB * CHUNK

A_WIDTH = 384
A_HEADS = 6
A_C = 8.0
B_WIDTH = 384
B_HEAD = 64
B_HEADS = 6
B_LORA = 128
B_COLS = 3 * B_WIDTH + B_LORA
B_GN_EPS = 64e-5
C_WIDTH = 256
C_HEADS = 4
C_HEAD = 64
C_LEVELS = 6
PAIR = 128
STAGGER = 8
A_OFF = 0
B_OFF = 2 * A_WIDTH
C_OFF = B_OFF + B_COLS
N_IN = C_OFF + 4 * C_WIDTH
D_MIX = A_WIDTH + B_WIDTH + C_WIDTH

V7X_VMEM_LIMIT = 56 * 1024 * 1024
ROW_TILE = 352


def _dot(a, b, ca=1, cb=0):
    dims = (((ca,), (cb,)), ((), ()))
    return lax.dot_general(a.astype(BF16), b.astype(BF16), dims, preferred_element_type=F32)


def _split2(x):
    hi = x.astype(BF16)
    return hi, (x - hi.astype(F32)).astype(BF16)


def _sel_left(m, x):
    n = x.shape[1]
    r = jnp.dot(m, jnp.concatenate(_split2(x), axis=1), preferred_element_type=F32)
    return r[:, :n] + r[:, n:]


def _bd(y, first):
    return jnp.concatenate([jnp.where(first, y, 0.0), jnp.where(first, 0.0, y)], axis=0)


def _sigmoid(x):
    return jax.nn.sigmoid(x)


def _softplus(x):
    return jnp.maximum(x, 0.0) + jnp.log1p(jnp.exp(-jnp.abs(x)))


def _rmsnorm(x, g):
    return x * lax.rsqrt(jnp.mean(x * x, axis=-1, keepdims=True) + NORM_EPS) * g


def _proj_kernel(h_ref, g_ref, w_ref, o_ref):
    xn = _rmsnorm(h_ref[...], g_ref[...])
    o_ref[...] = jnp.dot(xn.astype(BF16), w_ref[...], preferred_element_type=F32)


def _layer_spec(arr, layer):
    zeros = (0,) * (arr.ndim - 1)
    return pl.BlockSpec((None,) + arr.shape[1:], lambda *_: (layer,) + zeros)


def _proj(h2d, g, w_bf16, layer):
    rows, d = h2d.shape
    n = w_bf16.shape[2]
    return pl.pallas_call(
        _proj_kernel,
        grid=(rows // ROW_TILE,),
        in_specs=[pl.BlockSpec((ROW_TILE, d), lambda i: (i, 0)),
                  _layer_spec(g, layer), _layer_spec(w_bf16, layer)],
        out_specs=pl.BlockSpec((ROW_TILE, n), lambda i: (i, 0)),
        out_shape=jax.ShapeDtypeStruct((rows, n), F32),
        compiler_params=pltpu.CompilerParams(dimension_semantics=("arbitrary",),
                                             vmem_limit_bytes=V7X_VMEM_LIMIT),
        name="proj",
    )(h2d, g, w_bf16)


def _post_kernel(final, f_chunk, h_ref, y_ref, wo_ref, g_ref, wu_ref, wd_ref, gf_ref, o_ref):
    h1 = h_ref[...] + jnp.dot(y_ref[...].astype(BF16), wo_ref[...], preferred_element_type=F32)
    hb = _rmsnorm(h1, g_ref[...]).astype(BF16)
    acc = h1
    d_ff = wu_ref.shape[1]
    for c in range(d_ff // f_chunk):
        u = jnp.dot(hb, wu_ref[:, c * f_chunk:(c + 1) * f_chunk], preferred_element_type=F32)
        a = jnp.square(jnp.maximum(u, 0.0)).astype(BF16)
        acc = acc + jnp.dot(a, wd_ref[c * f_chunk:(c + 1) * f_chunk, :],
                            preferred_element_type=F32)
    if final:
        acc = _rmsnorm(acc, gf_ref[...])
    o_ref[...] = acc


def _post(h2d, y2d, wo, g, wu, wd, gf, layer, final):
    rows, d = h2d.shape
    tile = ROW_TILE
    return pl.pallas_call(
        functools.partial(_post_kernel, final, 1024),
        grid=(rows // tile,),
        in_specs=[pl.BlockSpec((tile, d), lambda i: (i, 0)),
                  pl.BlockSpec((tile, D_MIX), lambda i: (i, 0)),
                  _layer_spec(wo, layer), _layer_spec(g, layer),
                  _layer_spec(wu, layer), _layer_spec(wd, layer),
                  pl.BlockSpec(gf.shape, lambda i: (0, 0))],
        out_specs=pl.BlockSpec((tile, d), lambda i: (i, 0)),
        out_shape=jax.ShapeDtypeStruct((rows, d), F32),
        compiler_params=pltpu.CompilerParams(dimension_semantics=("arbitrary",),
                                             vmem_limit_bytes=V7X_VMEM_LIMIT),
        name="post",
    )(h2d, y2d, wo, g, wu, wd, gf)


def _rows(j):
    return slice(j * CHUNK, (j + 1) * CHUNK)


def _last_rows(x):
    return jnp.concatenate([jnp.broadcast_to(x[(j + 1) * CHUNK - 1:(j + 1) * CHUNK, :], (CHUNK, x.shape[1]))
                            for j in range(SUB)], axis=0)


def _group_a(bi, p_ref, o_ref, cw_ref, cb_ref, wri_ref, bri_ref, lam_ref, xbuf, h_a):
    t = STEP
    xa = p_ref[bi, :, A_OFF:A_OFF + A_WIDTH]
    ga = p_ref[bi, :, A_OFF + A_WIDTH:A_OFF + 2 * A_WIDTH]
    xbuf[bi, 8:8 + t, :] = xa
    u = cb_ref[...]
    for j in range(4):
        u = u + cw_ref[j:j + 1, :] * xbuf[bi, 5 + j:5 + j + t, :]
    xbuf[bi, 0:8, :] = xbuf[bi, t:t + 8, :]
    yield
    ri = _dot(u, wri_ref[...]) + bri_ref[...]
    r = _sigmoid(ri[:, :A_WIDTH])
    i = _sigmoid(ri[:, A_WIDTH:])
    yield
    log_a = (-A_C * r) * _softplus(-lam_ref[...])
    a = jnp.exp(log_a)
    om = 1.0 - a * a
    d = (om * lax.rsqrt(jnp.maximum(om, 1e-30))) * (i * u)
    yield
    rows = lax.broadcasted_iota(jnp.int32, (t, A_WIDTH), 0) & 7
    for s in (1, 2, 4):
        a_sh = jnp.where(rows < s, 1.0, pltpu.roll(a, s, 0))
        d_sh = jnp.where(rows < s, 0.0, pltpu.roll(d, s, 0))
        d = a * d_sh + d
        a = a * a_sh
        yield
    carry = h_a[bi, 0:1, :]
    hs = []
    for g in range(t // 8):
        hg = d[8 * g:8 * g + 8, :] + a[8 * g:8 * g + 8, :] * carry
        carry = hg[7:8, :]
        hs.append(hg)
        if g % 8 == 7:
            yield
    hseq = jnp.concatenate(hs, axis=0)
    h_a[bi, 0:1, :] = carry
    o_ref[bi, :, 0:A_WIDTH] = hseq * jax.nn.gelu(ga, approximate=True)


def _group_c(bi, p_ref, o_ref, lb_ref, cn_ref, tril, wlev, bo, bdm, s_c):
    t = CHUNK
    q_raw = p_ref[bi, :, C_OFF:C_OFF + C_WIDTH]
    fz = p_ref[bi, :, C_OFF + C_WIDTH:C_OFF + 2 * C_WIDTH]
    v = p_ref[bi, :, C_OFF + 2 * C_WIDTH:C_OFF + 3 * C_WIDTH]
    og = p_ref[bi, :, C_OFF + 3 * C_WIDTH:C_OFF + 4 * C_WIDTH]
    lb = lb_ref[...]
    f = lb + (1.0 - lb) * _sigmoid(fz)
    g = jnp.log(f)
    kd = 1.0 - f
    q = q_raw * _sigmoid(q_raw)
    yield
    b = _sel_left(tril, g)
    e_lev = jnp.exp(_sel_left(wlev, g))
    b_last = _last_rows(b)
    qe = q * jnp.exp(b)
    kl = kd * jnp.exp(b_last - b)
    g_last = jnp.exp(b_last)
    dq = _dot(q * kd, bo)
    yield
    ti = lax.broadcasted_iota(jnp.int32, (t, PAIR), 0)
    lane = lax.broadcasted_iota(jnp.int32, (t, PAIR), 1)
    first = lane < C_HEAD
    sj = lane & (C_HEAD - 1)
    xs = jnp.where(ti > sj, ti ^ sj, 0)
    npair = C_WIDTH // PAIR
    cs = [(j, p) for j in range(SUB) for p in range(npair)]
    sl = lambda x, c: x[_rows(c[0]), c[1] * PAIR:(c[1] + 1) * PAIR]
    att = [jnp.zeros((t, PAIR), F32) for _ in cs]
    for l in range(C_LEVELS):
        el = e_lev[l * STEP:(l + 1) * STEP, :]
        ql = q * el
        kl_l = kd * el
        lev = (xs >> l) == 1
        att = [att[i] + jnp.where(lev, _dot(sl(ql, c), _bd(sl(kl_l, c), first), 1, 1), 0.0)
               for i, c in enumerate(cs)]
        yield
    intra = [_dot(att[i], _bd(sl(v, c), first)) + sl(dq, c) * sl(v, c) for i, c in enumerate(cs)]
    upd = [jnp.where(bdm, _dot(sl(v, c), sl(kl, c), 0, 0), 0.0) for c in cs]
    yield
    sts = {}
    for p in range(npair):
        st = s_c[bi, p]
        for j in range(SUB):
            i = cs.index((j, p))
            sts[i] = st
            st = st * g_last[(j + 1) * t - 1:(j + 1) * t, p * PAIR:(p + 1) * PAIR] + upd[i]
        s_c[bi, p] = st
    outs = [intra[i] + _dot(sl(qe, c), sts[i], 1, 1) for i, c in enumerate(cs)]
    o = jnp.concatenate([jnp.concatenate([outs[cs.index((j, p))] for p in range(npair)], axis=1)
                         for j in range(SUB)], axis=0)
    yield
    ms = _dot(o * o, bo) * (1.0 / C_HEAD)
    o_ref[bi, :, A_WIDTH + B_WIDTH:D_MIX] = o * lax.rsqrt(ms + NORM_EPS) * cn_ref[...] * (og * _sigmoid(og))


def _group_b(bi, p_ref, o_ref, vf_ref, vfo_ref, mu_ref, wl_ref, w0_ref, a0_ref, kk_ref, ka_ref, rk_ref,
             lng_ref, lnb_ref, v0_ref, v1_ref, v2_ref, tril, bo, bdm, pbuf, s_b):
    t = CHUNK
    w = B_WIDTH
    pb = p_ref[bi, :, B_OFF:B_OFF + B_COLS]
    pbuf[bi, 8:8 + STEP, :] = pb
    prev = pbuf[bi, 7:7 + STEP, :]
    pbuf[bi, 0:8, :] = pbuf[bi, STEP:STEP + 8, :]
    sh = pb + mu_ref[...] * (prev - pb)
    r = sh[:, 0:w]
    k = sh[:, w:2 * w]
    v = sh[:, 2 * w:3 * w]
    lo = sh[:, 3 * w:3 * w + B_LORA]
    lane = lax.broadcasted_iota(jnp.int32, (STEP, B_LORA), 1)
    xl = jnp.where(lane < 32, jnp.tanh(lo), jnp.where(lane < 64, lo, _sigmoid(lo)))
    lora = _dot(xl, wl_ref[...])
    lw = -float(np.exp(-0.5)) * _sigmoid(w0_ref[...] + lora[:, 0:w])
    if vf_ref is None:
        vfo_ref[bi] = v
    else:
        mix = _sigmoid(v0_ref[...] + _dot(_dot(v, v1_ref[...]), v2_ref[...]))
        v = v + (vf_ref[bi] - v) * mix
    aa = _sigmoid(a0_ref[...] + lora[:, w:2 * w])
    gate = lora[:, 2 * w:3 * w]
    kk = k * kk_ref[...]
    kk = kk * lax.rsqrt(jnp.maximum(_dot(kk * kk, bo), 1e-24))
    k = k * (1.0 + (aa - 1.0) * ka_ref[...])
    ap = -kk
    bb = kk * aa
    c = _sel_left(tril, lw)
    c_last = _last_rows(c)
    a_h = ap * jnp.exp(c - lw)
    r_h = r * jnp.exp(c)
    e_inv = jnp.exp(-c)
    b_h = bb * e_inv
    k_h = k * e_inv
    e_cl = jnp.exp(c_last - c)
    b_c = bb * e_cl
    k_c = k * e_cl
    gam = jnp.exp(c_last)
    yield

    ti = lax.broadcasted_iota(jnp.int32, (t, PAIR), 0)
    lane = lax.broadcasted_iota(jnp.int32, (t, PAIR), 1)
    first = lane < B_HEAD
    sj = lane & (B_HEAD - 1)
    strict = ti > sj
    lower = ti >= sj
    blk = (ti >> 4) == (sj >> 4)
    eye = jnp.where(ti == sj, 1.0, 0.0)
    bd = lambda y: _bd(y, first)
    cat0 = lambda xs: jnp.concatenate(xs, axis=0)
    cat1 = lambda xs: jnp.concatenate(xs, axis=1)
    npair = B_WIDTH // PAIR
    cs = [(j, p) for j in range(SUB) for p in range(npair)]
    n = range(len(cs))
    sl = lambda x, i: x[_rows(cs[i][0]), cs[i][1] * PAIR:(cs[i][1] + 1) * PAIR]
    vbd = [bd(sl(v, i)) for i in n]
    ar = [cat0([sl(a_h, i), sl(r_h, i)]) for i in n]
    mm = [_dot(ar[i], cat0([bd(sl(b_h, i)), bd(sl(k_h, i))]), 1, 1) for i in n]
    yield
    m_ab = [jnp.where(strict, m[:t, :PAIR], 0.0) for m in mm]
    m_ak = [jnp.where(strict, m[:t, PAIR:], 0.0) for m in mm]
    n_rb = [jnp.where(lower, m[t:, :PAIR], 0.0) for m in mm]
    n_rk = [jnp.where(lower, m[t:, PAIR:], 0.0) for m in mm]
    dg = [jnp.where(blk, m, 0.0) for m in m_ab]
    off = [m_ab[i] - dg[i] for i in n]
    akv = [_dot(m_ak[i], vbd[i]) for i in n]
    nkv = [_dot(n_rk[i], vbd[i]) for i in n]
    d2 = [_dot(dg[i], bd(dg[i])) for i in n]
    yield
    s4 = [_dot(cat0([d2[i], dg[i]]), bd(d2[i])) for i in n]
    yield
    d4 = [x[:t] for x in s4]
    x1 = [eye + dg[i] + d2[i] + s4[i][t:] for i in n]
    s8 = [_dot(cat0([d4[i], x1[i]]), bd(d4[i])) for i in n]
    yield
    x2 = [x1[i] + s8[i][t:] for i in n]
    dinv = [x2[i] + _dot(x2[i], bd(s8[i][:t])) for i in n]
    yield
    gm = [_dot(dinv[i], bd(off[i])) for i in n]
    yield
    g2 = [_dot(gm[i], bd(gm[i])) for i in n]
    yield
    gs = [eye + gm[i] + g2[i] + _dot(gm[i], bd(g2[i])) for i in n]
    yield
    tinv = [_dot(gs[i], bd(dinv[i])) for i in n]
    yield
    pq = [_dot(tinv[i], cat1([bd(sl(a_h, i)), bd(akv[i])])) for i in n]
    yield
    npq = [_dot(n_rb[i], cat1([bd(pq[i][:, :PAIR]), bd(pq[i][:, PAIR:])])) for i in n]
    yield
    p2 = [sl(r_h, i) + npq[i][:, :PAIR] for i in n]
    q2 = [npq[i][:, PAIR:] + nkv[i] for i in n]
    pqb = [_dot(pq[i], sl(b_c, i), 0, 0) for i in n]
    vk = [_dot(sl(v, i), sl(k_c, i), 0, 0) for i in n]
    yield
    wp = [jnp.where(bdm, x[:PAIR], 0.0) for x in pqb]
    zz = [jnp.where(bdm, pqb[i][PAIR:] + vk[i], 0.0) for i in n]
    sts = {}
    for p in range(npair):
        st = s_b[bi, p]
        for j in range(SUB):
            i = cs.index((j, p))
            sts[i] = st
            st = st * gam[(j + 1) * t - 1:(j + 1) * t, p * PAIR:(p + 1) * PAIR] + _dot(st, wp[i]) + zz[i]
        s_b[bi, p] = st
    ys = [_dot(p2[i], sts[i], 1, 1) + q2[i] for i in n]
    yield
    y = cat0([cat1([ys[cs.index((j, p))] for p in range(npair)]) for j in range(SUB)])
    inv_n = 1.0 / B_HEAD
    mean = _dot(y, bo) * inv_n
    yc = y - mean
    var = _dot(yc * yc, bo) * inv_n
    y = yc * lax.rsqrt(var + B_GN_EPS) * lng_ref[...] + lnb_ref[...]
    y = y + _dot(r * k * rk_ref[...], bo) * v
    o_ref[bi, :, A_WIDTH:A_WIDTH + B_WIDTH] = y * gate


def _mixer_kernel(layer0, *refs):
    it = iter(refs)
    p_ref = next(it)
    vf_ref = None if layer0 else next(it)
    cw, cb, wri, bri, lam = (next(it) for _ in range(5))
    mu, wl, w0, a0, kkp, kap, rkp, lng, lnb = (next(it) for _ in range(9))
    v0 = v1 = v2 = None
    if not layer0:
        v0, v1, v2 = (next(it) for _ in range(3))
    lb, cn, tril_ref, wlev_ref, bo_ref = (next(it) for _ in range(5))
    y_ref = next(it)
    vfo_ref = next(it) if layer0 else None
    xbuf, h_a, pbuf, s_b, s_c = (next(it) for _ in range(5))

    @pl.when(pl.program_id(1) == 0)
    def _():
        xbuf[:, 0:8, :] = jnp.zeros((NBATCH, 8, A_WIDTH), F32)
        pbuf[:, 0:8, :] = jnp.zeros((NBATCH, 8, B_COLS), F32)
        h_a[...] = jnp.zeros_like(h_a)
        s_b[...] = jnp.zeros_like(s_b)
        s_c[...] = jnp.zeros_like(s_c)

    tril = tril_ref[...]
    bo = bo_ref[...]
    half = lambda d: lax.broadcasted_iota(jnp.int32, (PAIR, PAIR), d) < PAIR // 2
    bdm = half(0) == half(1)
    wlev = wlev_ref[...]
    bo_c = bo[:C_WIDTH, :C_WIDTH]
    live = []
    for bi in range(NBATCH):
        gb = _group_b(bi, p_ref, y_ref, vf_ref, vfo_ref, mu, wl, w0, a0, kkp, kap, rkp, lng, lnb, v0, v1, v2,
                      tril, bo, bdm, pbuf, s_b)
        gc = _group_c(bi, p_ref, y_ref, lb, cn, tril, wlev, bo_c, bdm, s_c)
        ga = _group_a(bi, p_ref, y_ref, cw, cb, wri, bri, lam, xbuf, h_a)
        next(gb)
        live += [gb, gc, ga]
        rounds = STAGGER if bi + 1 < NBATCH else 1 << 30
        while live and rounds:
            rounds -= 1
            for gen in list(live):
                try:
                    next(gen)
                except StopIteration:
                    live.remove(gen)


def _level_matrix():
    t = CHUNK
    w = np.zeros((C_LEVELS, t, t), np.float32)
    for l in range(C_LEVELS):
        half = 1 << l
        for row in range(t):
            base = (row >> (l + 1)) << (l + 1)
            if (row >> l) & 1:
                w[l, row, base + half:row + 1] = 1.0
            else:
                w[l, row, row + 1:base + half] = 1.0
    return w


def _step_constants():
    eye = np.eye(SUB, dtype=np.float32)
    tril = np.kron(eye, np.tril(np.ones((CHUNK, CHUNK), np.float32)))
    wlev = np.concatenate([np.kron(eye, lv) for lv in _level_matrix()], axis=0)
    hid = np.arange(B_WIDTH) // B_HEAD
    bo = (hid[:, None] == hid[None, :]).astype(np.float32)
    return tuple(jnp.asarray(m, BF16) for m in (tril, wlev, bo))


def _mixer(proj, v_first, params, consts, layer0):
    bsz, lp, _ = proj.shape
    t = STEP
    const = lambda b, c: (0, 0)
    row = lambda b, c: (b, c, 0)
    ins = [proj]
    specs = [pl.BlockSpec((NBATCH, t, N_IN), row)]
    if not layer0:
        ins.append(v_first)
        specs.append(pl.BlockSpec((NBATCH, t, B_WIDTH), row))
    for arr, layer in params:
        ins.append(arr)
        specs.append(_layer_spec(arr, layer))
    for arr in consts:
        ins.append(arr)
        specs.append(pl.BlockSpec(arr.shape, const))
    out_shape = [jax.ShapeDtypeStruct((bsz, lp, D_MIX), F32)]
    out_specs = [pl.BlockSpec((NBATCH, t, D_MIX), row)]
    if layer0:
        out_shape.append(jax.ShapeDtypeStruct((bsz, lp, B_WIDTH), F32))
        out_specs.append(pl.BlockSpec((NBATCH, t, B_WIDTH), row))
    res = pl.pallas_call(
        functools.partial(_mixer_kernel, layer0),
        grid=(bsz // NBATCH, lp // t),
        in_specs=specs,
        out_specs=out_specs,
        out_shape=out_shape,
        scratch_shapes=[pltpu.VMEM((NBATCH, t + 8, A_WIDTH), F32),
                        pltpu.VMEM((NBATCH, 8, A_WIDTH), F32),
                        pltpu.VMEM((NBATCH, t + 8, B_COLS), F32),
                        pltpu.VMEM((NBATCH, B_WIDTH // PAIR, PAIR, PAIR), F32),
                        pltpu.VMEM((NBATCH, C_WIDTH // PAIR, PAIR, PAIR), F32)],
        compiler_params=pltpu.CompilerParams(dimension_semantics=("arbitrary", "arbitrary"),
                                             vmem_limit_bytes=V7X_VMEM_LIMIT),
        name="mixer0" if layer0 else "mixer",
    )(*ins)
    return res if layer0 else (res[0], v_first)


def _block_diag(w):
    nl, hn, n, _ = w.shape
    out = jnp.zeros((nl, hn * n, hn * n), w.dtype)
    for h in range(hn):
        out = out.at[:, h * n:(h + 1) * n, h * n:(h + 1) * n].set(w[:, h])
    return out


def kernel(x, meta, mix_norm, w_in, a_conv_w, a_conv_b, a_w_r, a_b_r, a_w_i, a_b_i, a_lambda, b_mu, b_w0, b_w2, b_a0, b_a2, b_g2, b_k_k, b_k_a, b_r_k, b_ln_g, b_ln_b, b_v0, b_v1, b_v2, c_lb, c_norm, w_out, ffn_norm, w_up, w_down, final_norm):
    bsz, seq, d = x.shape
    depth = w_in.shape[0]
    l_real = N_META + seq
    lp = -(-l_real // STEP) * STEP
    while (bsz * lp) % ROW_TILE:
        lp += STEP
    h = jnp.concatenate([jnp.broadcast_to(meta[None].astype(x.dtype), (bsz, N_META, d)), x,
                         jnp.zeros((bsz, lp - l_real, d), x.dtype)], axis=1)
    h = h.reshape(bsz * lp, d)

    lb_all = jnp.cumsum(jax.nn.softmax(c_lb.astype(F32), axis=0), axis=0)
    lb_all = lb_all - lb_all[0]
    consts = _step_constants()
    row = lambda a: a.reshape(a.shape[0], 1, -1)
    wri = jnp.concatenate([_block_diag(a_w_r), _block_diag(a_w_i)], axis=2).astype(BF16)
    bri = row(jnp.concatenate([a_b_r, a_b_i], axis=1))
    wl = jnp.zeros((depth, B_LORA, 3 * B_WIDTH), F32)
    wl = wl.at[:, 0:32, 0:B_WIDTH].set(b_w2)
    wl = wl.at[:, 32:64, B_WIDTH:2 * B_WIDTH].set(b_a2)
    wl = wl.at[:, 64:128, 2 * B_WIDTH:].set(b_g2).astype(BF16)
    shared = [a_conv_w, row(a_conv_b), wri, bri, row(a_lambda), row(b_mu), wl, row(b_w0), row(b_a0),
              row(b_k_k), row(b_k_a), row(b_r_k), row(b_ln_g), row(b_ln_b)]
    vmix = [row(b_v0), b_v1.astype(BF16), b_v2.astype(BF16)]
    tail = [row(lb_all), row(jnp.tile(c_norm, (1, C_HEADS)))]
    g_mix, g_ffn = row(mix_norm), row(ffn_norm)
    w_in_b, w_out_b, w_up_b, w_down_b = (a.astype(BF16) for a in (w_in, w_out, w_up, w_down))

    v_first = None
    for l in range(depth):
        proj = _proj(h, g_mix, w_in_b, l).reshape(bsz, lp, N_IN)
        params = [(a, l) for a in shared] + ([(a, l - 1) for a in vmix] if l else []) + [(a, l) for a in tail]
        y, v_first = _mixer(proj, v_first, params, consts, l == 0)
        h = _post(h, y.reshape(bsz * lp, D_MIX), w_out_b, g_ffn, w_up_b, w_down_b,
                  final_norm.reshape(1, -1), l, l == depth - 1)
    return h.reshape(bsz, lp, d)[:, N_META:l_real]
```

```python
import functools

import numpy as np
import jax
import jax.numpy as jnp
from jax import lax
from jax.experimental import pallas as pl
from jax.experimental.pallas import tpu as pltpu

F32 = jnp.float32
BF16 = jnp.bfloat16

N_META = 16
NORM_EPS = 1e-6
CHUNK = 64
SUB = 3
NBATCH = 2
STEP = SUB * CHUNK

A_WIDTH = 384
A_HEADS = 6
A_C = 8.0
B_WIDTH = 384
B_HEAD = 64
B_HEADS = 6
B_LORA = 128
B_COLS = 3 * B_WIDTH + B_LORA
B_GN_EPS = 64e-5
C_WIDTH = 256
C_HEADS = 4
C_HEAD = 64
C_LEVELS = 6
PAIR = 128
STAGGER = 4
A_OFF = 0
B_OFF = 2 * A_WIDTH
C_OFF = B_OFF + B_COLS
N_IN = C_OFF + 4 * C_WIDTH
D_MIX = A_WIDTH + B_WIDTH + C_WIDTH

V7X_VMEM_LIMIT = 56 * 1024 * 1024
ROW_TILE = 704


def _dot(a, b, ca=1, cb=0):
    dims = (((ca,), (cb,)), ((), ()))
    return lax.dot_general(a.astype(BF16), b.astype(BF16), dims, preferred_element_type=F32)


def _split2(x):
    hi = x.astype(BF16)
    return hi, (x - hi.astype(F32)).astype(BF16)


def _sel_left(m, x):
    n = x.shape[1]
    r = jnp.dot(m, jnp.concatenate(_split2(x), axis=1), preferred_element_type=F32)
    return r[:, :n] + r[:, n:]


def _bd(y, first):
    return jnp.concatenate([jnp.where(first, y, 0.0), jnp.where(first, 0.0, y)], axis=0)


def _sigmoid(x):
    return jax.nn.sigmoid(x)


def _softplus(x):
    return jnp.maximum(x, 0.0) + jnp.log1p(jnp.exp(-jnp.abs(x)))


def _rmsnorm(x, g):
    return x * lax.rsqrt(jnp.mean(x * x, axis=-1, keepdims=True) + NORM_EPS) * g


def _proj_kernel(h_ref, g_ref, w_ref, o_ref, wb):
    @pl.when(pl.program_id(0) == 0)
    def _():
        wb[...] = w_ref[...].astype(BF16)

    xn = _rmsnorm(h_ref[...], g_ref[...])
    o_ref[...] = jnp.dot(xn.astype(BF16), wb[...], preferred_element_type=F32)


def _layer_spec(arr, layer, single=False):
    zeros = (0,) * (arr.ndim - 1)
    kw = dict(pipeline_mode=pl.Buffered(1)) if single else {}
    return pl.BlockSpec((None,) + arr.shape[1:], lambda *_: (layer,) + zeros, **kw)


def _proj(h2d, g, w, layer):
    rows, d = h2d.shape
    n = w.shape[2]
    return pl.pallas_call(
        _proj_kernel,
        grid=(rows // ROW_TILE,),
        in_specs=[pl.BlockSpec((ROW_TILE, d), lambda i: (i, 0)),
                  _layer_spec(g, layer), _layer_spec(w, layer, single=True)],
        out_specs=pl.BlockSpec((ROW_TILE, n), lambda i: (i, 0)),
        out_shape=jax.ShapeDtypeStruct((rows, n), F32),
        scratch_shapes=[pltpu.VMEM((d, n), BF16)],
        compiler_params=pltpu.CompilerParams(dimension_semantics=("arbitrary",),
                                             vmem_limit_bytes=V7X_VMEM_LIMIT),
        name="proj",
    )(h2d, g, w)


def _post_kernel(final, f_chunk, h_ref, y_ref, wo_ref, g_ref, wu_ref, wd_ref, gf_ref, o_ref, wob):
    @pl.when(pl.program_id(0) == 0)
    def _():
        wob[...] = wo_ref[...].astype(BF16)

    h1 = h_ref[...] + jnp.dot(y_ref[...].astype(BF16), wob[...], preferred_element_type=F32)
    hb = _rmsnorm(h1, g_ref[...]).astype(BF16)
    acc = h1
    d_ff = wu_ref.shape[1]
    for c in range(d_ff // f_chunk):
        u = jnp.dot(hb, wu_ref[:, c * f_chunk:(c + 1) * f_chunk], preferred_element_type=F32)
        a = jnp.square(jnp.maximum(u, 0.0)).astype(BF16)
        acc = acc + jnp.dot(a, wd_ref[c * f_chunk:(c + 1) * f_chunk, :],
                            preferred_element_type=F32)
    if final:
        acc = _rmsnorm(acc, gf_ref[...])
    o_ref[...] = acc


def _post(h2d, y2d, wo, g, wu, wd, gf, layer, final):
    rows, d = h2d.shape
    tile = ROW_TILE
    return pl.pallas_call(
        functools.partial(_post_kernel, final, 1024),
        grid=(rows // tile,),
        in_specs=[pl.BlockSpec((tile, d), lambda i: (i, 0)),
                  pl.BlockSpec((tile, D_MIX), lambda i: (i, 0)),
                  _layer_spec(wo, layer, single=True), _layer_spec(g, layer),
                  _layer_spec(wu, layer, single=True), _layer_spec(wd, layer, single=True),
                  pl.BlockSpec(gf.shape, lambda i: (0, 0))],
        out_specs=pl.BlockSpec((tile, d), lambda i: (i, 0)),
        out_shape=jax.ShapeDtypeStruct((rows, d), F32),
        scratch_shapes=[pltpu.VMEM((D_MIX, d), BF16)],
        compiler_params=pltpu.CompilerParams(dimension_semantics=("arbitrary",),
                                             vmem_limit_bytes=V7X_VMEM_LIMIT),
        name="post",
    )(h2d, y2d, wo, g, wu, wd, gf)


def _rows(j):
    return slice(j * CHUNK, (j + 1) * CHUNK)


def _last_rows(x):
    return jnp.concatenate([jnp.broadcast_to(x[(j + 1) * CHUNK - 1:(j + 1) * CHUNK, :], (CHUNK, x.shape[1]))
                            for j in range(SUB)], axis=0)


def _group_a(bi, p_ref, o_ref, cw_ref, cb_ref, wri_ref, bri_ref, lam_ref, xbuf, h_a):
    t = STEP
    xa = p_ref[bi, :, A_OFF:A_OFF + A_WIDTH]
    ga = p_ref[bi, :, A_OFF + A_WIDTH:A_OFF + 2 * A_WIDTH]
    xbuf[bi, 8:8 + t, :] = xa
    u = cb_ref[...]
    for j in range(4):
        u = u + cw_ref[j:j + 1, :] * xbuf[bi, 5 + j:5 + j + t, :]
    xbuf[bi, 0:8, :] = xbuf[bi, t:t + 8, :]
    yield
    ri = _dot(u, wri_ref[...]) + bri_ref[...]
    r = _sigmoid(ri[:, :A_WIDTH])
    i = _sigmoid(ri[:, A_WIDTH:])
    yield
    log_a = (-A_C * r) * _softplus(-lam_ref[...])
    a = jnp.exp(log_a)
    om = 1.0 - a * a
    d = (om * lax.rsqrt(jnp.maximum(om, 1e-30))) * (i * u)
    yield
    rows = lax.broadcasted_iota(jnp.int32, (t, A_WIDTH), 0) & 7
    for s in (1, 2, 4):
        a_sh = jnp.where(rows < s, 1.0, pltpu.roll(a, s, 0))
        d_sh = jnp.where(rows < s, 0.0, pltpu.roll(d, s, 0))
        d = a * d_sh + d
        a = a * a_sh
        yield
    carry = h_a[bi, 0:1, :]
    hs = []
    for g in range(t // 8):
        hg = d[8 * g:8 * g + 8, :] + a[8 * g:8 * g + 8, :] * carry
        carry = hg[7:8, :]
        hs.append(hg)
        if g % 8 == 7:
            yield
    hseq = jnp.concatenate(hs, axis=0)
    h_a[bi, 0:1, :] = carry
    o_ref[bi, :, 0:A_WIDTH] = hseq * jax.nn.gelu(ga, approximate=True)


def _group_c(bi, p_ref, o_ref, lb_ref, cn_ref, tril, wlev, bo, bdm, s_c):
    t = CHUNK
    q_raw = p_ref[bi, :, C_OFF:C_OFF + C_WIDTH]
    fz = p_ref[bi, :, C_OFF + C_WIDTH:C_OFF + 2 * C_WIDTH]
    v = p_ref[bi, :, C_OFF + 2 * C_WIDTH:C_OFF + 3 * C_WIDTH]
    og = p_ref[bi, :, C_OFF + 3 * C_WIDTH:C_OFF + 4 * C_WIDTH]
    lb = lb_ref[...]
    f = lb + (1.0 - lb) * _sigmoid(fz)
    g = jnp.log(f)
    kd = 1.0 - f
    q = q_raw * _sigmoid(q_raw)
    yield
    b = _sel_left(tril, g)
    e_lev = jnp.exp(_sel_left(wlev, g))
    b_last = _last_rows(b)
    qe = q * jnp.exp(b)
    kl = kd * jnp.exp(b_last - b)
    g_last = jnp.exp(b_last)
    dq = _dot(q * kd, bo)
    yield
    ti = lax.broadcasted_iota(jnp.int32, (t, PAIR), 0)
    lane = lax.broadcasted_iota(jnp.int32, (t, PAIR), 1)
    first = lane < C_HEAD
    sj = lane & (C_HEAD - 1)
    xs = jnp.where(ti > sj, ti ^ sj, 0)
    npair = C_WIDTH // PAIR
    cs = [(j, p) for j in range(SUB) for p in range(npair)]
    sl = lambda x, c: x[_rows(c[0]), c[1] * PAIR:(c[1] + 1) * PAIR]
    att = [jnp.zeros((t, PAIR), F32) for _ in cs]
    for l in range(C_LEVELS):
        el = e_lev[l * STEP:(l + 1) * STEP, :]
        ql = q * el
        kl_l = kd * el
        lev = (xs >> l) == 1
        att = [att[i] + jnp.where(lev, _dot(sl(ql, c), _bd(sl(kl_l, c), first), 1, 1), 0.0)
               for i, c in enumerate(cs)]
        yield
    intra = [_dot(att[i], _bd(sl(v, c), first)) + sl(dq, c) * sl(v, c) for i, c in enumerate(cs)]
    upd = [jnp.where(bdm, _dot(sl(v, c), sl(kl, c), 0, 0), 0.0) for c in cs]
    yield
    sts = {}
    for p in range(npair):
        st = s_c[bi, p]
        for j in range(SUB):
            i = cs.index((j, p))
            sts[i] = st
            st = st * g_last[(j + 1) * t - 1:(j + 1) * t, p * PAIR:(p + 1) * PAIR] + upd[i]
        s_c[bi, p] = st
    outs = [intra[i] + _dot(sl(qe, c), sts[i], 1, 1) for i, c in enumerate(cs)]
    o = jnp.concatenate([jnp.concatenate([outs[cs.index((j, p))] for p in range(npair)], axis=1)
                         for j in range(SUB)], axis=0)
    yield
    ms = _dot(o * o, bo) * (1.0 / C_HEAD)
    o_ref[bi, :, A_WIDTH + B_WIDTH:D_MIX] = o * lax.rsqrt(ms + NORM_EPS) * cn_ref[...] * (og * _sigmoid(og))


def _group_b(bi, p_ref, o_ref, vf_ref, vfo_ref, mu_ref, wl_ref, w0_ref, a0_ref, kk_ref, ka_ref, rk_ref,
             lng_ref, lnb_ref, v0_ref, v1_ref, v2_ref, tril, bo, bdm, pbuf, s_b):
    t = CHUNK
    w = B_WIDTH
    pb = p_ref[bi, :, B_OFF:B_OFF + B_COLS]
    pbuf[bi, 8:8 + STEP, :] = pb
    prev = pbuf[bi, 7:7 + STEP, :]
    pbuf[bi, 0:8, :] = pbuf[bi, STEP:STEP + 8, :]
    sh = pb + mu_ref[...] * (prev - pb)
    r = sh[:, 0:w]
    k = sh[:, w:2 * w]
    v = sh[:, 2 * w:3 * w]
    lo = sh[:, 3 * w:3 * w + B_LORA]
    lane = lax.broadcasted_iota(jnp.int32, (STEP, B_LORA), 1)
    xl = jnp.where(lane < 32, jnp.tanh(lo), jnp.where(lane < 64, lo, _sigmoid(lo)))
    lora = _dot(xl, wl_ref[...])
    lw = -float(np.exp(-0.5)) * _sigmoid(w0_ref[...] + lora[:, 0:w])
    if vf_ref is None:
        vfo_ref[bi] = v
    else:
        mix = _sigmoid(v0_ref[...] + _dot(_dot(v, v1_ref[...]), v2_ref[...]))
        v = v + (vf_ref[bi] - v) * mix
    aa = _sigmoid(a0_ref[...] + lora[:, w:2 * w])
    gate = lora[:, 2 * w:3 * w]
    kk = k * kk_ref[...]
    kk = kk * lax.rsqrt(jnp.maximum(_dot(kk * kk, bo), 1e-24))
    k = k * (1.0 + (aa - 1.0) * ka_ref[...])
    ap = -kk
    bb = kk * aa
    c = _sel_left(tril, lw)
    c_last = _last_rows(c)
    a_h = ap * jnp.exp(c - lw)
    r_h = r * jnp.exp(c)
    e_inv = jnp.exp(-c)
    b_h = bb * e_inv
    k_h = k * e_inv
    e_cl = jnp.exp(c_last - c)
    b_c = bb * e_cl
    k_c = k * e_cl
    gam = jnp.exp(c_last)
    yield

    ti = lax.broadcasted_iota(jnp.int32, (t, PAIR), 0)
    lane = lax.broadcasted_iota(jnp.int32, (t, PAIR), 1)
    first = lane < B_HEAD
    sj = lane & (B_HEAD - 1)
    strict = ti > sj
    lower = ti >= sj
    blk = (ti >> 4) == (sj >> 4)
    eye = jnp.where(ti == sj, 1.0, 0.0)
    bd = lambda y: _bd(y, first)
    cat0 = lambda xs: jnp.concatenate(xs, axis=0)
    cat1 = lambda xs: jnp.concatenate(xs, axis=1)
    npair = B_WIDTH // PAIR
    cs = [(j, p) for j in range(SUB) for p in range(npair)]
    n = range(len(cs))
    sl = lambda x, i: x[_rows(cs[i][0]), cs[i][1] * PAIR:(cs[i][1] + 1) * PAIR]
    vbd = [bd(sl(v, i)) for i in n]
    ar = [cat0([sl(a_h, i), sl(r_h, i)]) for i in n]
    mm = [_dot(ar[i], cat0([bd(sl(b_h, i)), bd(sl(k_h, i))]), 1, 1) for i in n]
    yield
    m_ab = [jnp.where(strict, m[:t, :PAIR], 0.0) for m in mm]
    m_ak = [jnp.where(strict, m[:t, PAIR:], 0.0) for m in mm]
    n_rb = [jnp.where(lower, m[t:, :PAIR], 0.0) for m in mm]
    n_rk = [jnp.where(lower, m[t:, PAIR:], 0.0) for m in mm]
    dg = [jnp.where(blk, m, 0.0) for m in m_ab]
    off = [m_ab[i] - dg[i] for i in n]
    akv = [_dot(m_ak[i], vbd[i]) for i in n]
    nkv = [_dot(n_rk[i], vbd[i]) for i in n]
    d2 = [_dot(dg[i], bd(dg[i])) for i in n]
    yield
    s4 = [_dot(cat0([d2[i], dg[i]]), bd(d2[i])) for i in n]
    yield
    d4 = [x[:t] for x in s4]
    x1 = [eye + dg[i] + d2[i] + s4[i][t:] for i in n]
    s8 = [_dot(cat0([d4[i], x1[i]]), bd(d4[i])) for i in n]
    yield
    x2 = [x1[i] + s8[i][t:] for i in n]
    dinv = [x2[i] + _dot(x2[i], bd(s8[i][:t])) for i in n]
    yield
    gm = [_dot(dinv[i], bd(off[i])) for i in n]
    yield
    g2 = [_dot(gm[i], bd(gm[i])) for i in n]
    yield
    gs = [eye + gm[i] + g2[i] + _dot(gm[i], bd(g2[i])) for i in n]
    yield
    tinv = [_dot(gs[i], bd(dinv[i])) for i in n]
    yield
    pq = [_dot(tinv[i], cat1([bd(sl(a_h, i)), bd(akv[i])])) for i in n]
    yield
    npq = [_dot(n_rb[i], cat1([bd(pq[i][:, :PAIR]), bd(pq[i][:, PAIR:])])) for i in n]
    yield
    p2 = [sl(r_h, i) + npq[i][:, :PAIR] for i in n]
    q2 = [npq[i][:, PAIR:] + nkv[i] for i in n]
    pqb = [_dot(pq[i], sl(b_c, i), 0, 0) for i in n]
    vk = [_dot(sl(v, i), sl(k_c, i), 0, 0) for i in n]
    yield
    wp = [jnp.where(bdm, x[:PAIR], 0.0) for x in pqb]
    zz = [jnp.where(bdm, pqb[i][PAIR:] + vk[i], 0.0) for i in n]
    sts = {}
    for p in range(npair):
        st = s_b[bi, p]
        for j in range(SUB):
            i = cs.index((j, p))
            sts[i] = st
            st = st * gam[(j + 1) * t - 1:(j + 1) * t, p * PAIR:(p + 1) * PAIR] + _dot(st, wp[i]) + zz[i]
        s_b[bi, p] = st
    ys = [_dot(p2[i], sts[i], 1, 1) + q2[i] for i in n]
    yield
    y = cat0([cat1([ys[cs.index((j, p))] for p in range(npair)]) for j in range(SUB)])
    inv_n = 1.0 / B_HEAD
    mean = _dot(y, bo) * inv_n
    yc = y - mean
    var = _dot(yc * yc, bo) * inv_n
    y = yc * lax.rsqrt(var + B_GN_EPS) * lng_ref[...] + lnb_ref[...]
    y = y + _dot(r * k * rk_ref[...], bo) * v
    o_ref[bi, :, A_WIDTH:A_WIDTH + B_WIDTH] = y * gate


def _mixer_kernel(layer0, *refs):
    it = iter(refs)
    p_ref = next(it)
    vf_ref = None if layer0 else next(it)
    cw, cb, wri, bri, lam = (next(it) for _ in range(5))
    mu, wl, w0, a0, kkp, kap, rkp, lng, lnb = (next(it) for _ in range(9))
    v0 = v1 = v2 = None
    if not layer0:
        v0, v1, v2 = (next(it) for _ in range(3))
    lb, cn, tril_ref, wlev_ref, bo_ref = (next(it) for _ in range(5))
    y_ref = next(it)
    vfo_ref = next(it) if layer0 else None
    xbuf, h_a, pbuf, s_b, s_c = (next(it) for _ in range(5))

    @pl.when(pl.program_id(1) == 0)
    def _():
        xbuf[:, 0:8, :] = jnp.zeros((NBATCH, 8, A_WIDTH), F32)
        pbuf[:, 0:8, :] = jnp.zeros((NBATCH, 8, B_COLS), F32)
        h_a[...] = jnp.zeros_like(h_a)
        s_b[...] = jnp.zeros_like(s_b)
        s_c[...] = jnp.zeros_like(s_c)

    tril = tril_ref[...]
    bo = bo_ref[...]
    half = lambda d: lax.broadcasted_iota(jnp.int32, (PAIR, PAIR), d) < PAIR // 2
    bdm = half(0) == half(1)
    wlev = wlev_ref[...]
    bo_c = bo[:C_WIDTH, :C_WIDTH]
    live = []
    for bi in range(NBATCH):
        gb = _group_b(bi, p_ref, y_ref, vf_ref, vfo_ref, mu, wl, w0, a0, kkp, kap, rkp, lng, lnb, v0, v1, v2,
                      tril, bo, bdm, pbuf, s_b)
        gc = _group_c(bi, p_ref, y_ref, lb, cn, tril, wlev, bo_c, bdm, s_c)
        ga = _group_a(bi, p_ref, y_ref, cw, cb, wri, bri, lam, xbuf, h_a)
        next(gb)
        live += [gb, gc, ga]
        rounds = STAGGER if bi + 1 < NBATCH else 1 << 30
        while live and rounds:
            rounds -= 1
            for gen in list(live):
                try:
                    next(gen)
                except StopIteration:
                    live.remove(gen)


def _level_matrix():
    t = CHUNK
    w = np.zeros((C_LEVELS, t, t), np.float32)
    for l in range(C_LEVELS):
        half = 1 << l
        for row in range(t):
            base = (row >> (l + 1)) << (l + 1)
            if (row >> l) & 1:
                w[l, row, base + half:row + 1] = 1.0
            else:
                w[l, row, row + 1:base + half] = 1.0
    return w


def _step_constants():
    eye = np.eye(SUB, dtype=np.float32)
    tril = np.kron(eye, np.tril(np.ones((CHUNK, CHUNK), np.float32)))
    wlev = np.concatenate([np.kron(eye, lv) for lv in _level_matrix()], axis=0)
    hid = np.arange(B_WIDTH) // B_HEAD
    bo = (hid[:, None] == hid[None, :]).astype(np.float32)
    return tuple(jnp.asarray(m, BF16) for m in (tril, wlev, bo))


def _mixer(proj, v_first, params, consts, layer0):
    bsz, lp, _ = proj.shape
    t = STEP
    const = lambda b, c: (0, 0)
    row = lambda b, c: (b, c, 0)
    ins = [proj]
    specs = [pl.BlockSpec((NBATCH, t, N_IN), row)]
    if not layer0:
        ins.append(v_first)
        specs.append(pl.BlockSpec((NBATCH, t, B_WIDTH), row))
    for arr, layer in params:
        ins.append(arr)
        specs.append(_layer_spec(arr, layer))
    for arr in consts:
        ins.append(arr)
        specs.append(pl.BlockSpec(arr.shape, const))
    out_shape = [jax.ShapeDtypeStruct((bsz, lp, D_MIX), F32)]
    out_specs = [pl.BlockSpec((NBATCH, t, D_MIX), row)]
    if layer0:
        out_shape.append(jax.ShapeDtypeStruct((bsz, lp, B_WIDTH), F32))
        out_specs.append(pl.BlockSpec((NBATCH, t, B_WIDTH), row))
    res = pl.pallas_call(
        functools.partial(_mixer_kernel, layer0),
        grid=(bsz // NBATCH, lp // t),
        in_specs=specs,
        out_specs=out_specs,
        out_shape=out_shape,
        scratch_shapes=[pltpu.VMEM((NBATCH, t + 8, A_WIDTH), F32),
                        pltpu.VMEM((NBATCH, 8, A_WIDTH), F32),
                        pltpu.VMEM((NBATCH, t + 8, B_COLS), F32),
                        pltpu.VMEM((NBATCH, B_WIDTH // PAIR, PAIR, PAIR), F32),
                        pltpu.VMEM((NBATCH, C_WIDTH // PAIR, PAIR, PAIR), F32)],
        compiler_params=pltpu.CompilerParams(dimension_semantics=("arbitrary", "arbitrary"),
                                             vmem_limit_bytes=V7X_VMEM_LIMIT),
        name="mixer0" if layer0 else "mixer",
    )(*ins)
    return res if layer0 else (res[0], v_first)


def _block_diag(w):
    nl, hn, n, _ = w.shape
    out = jnp.zeros((nl, hn * n, hn * n), w.dtype)
    for h in range(hn):
        out = out.at[:, h * n:(h + 1) * n, h * n:(h + 1) * n].set(w[:, h])
    return out


def kernel(x, meta, mix_norm, w_in, a_conv_w, a_conv_b, a_w_r, a_b_r, a_w_i, a_b_i, a_lambda, b_mu, b_w0, b_w2, b_a0, b_a2, b_g2, b_k_k, b_k_a, b_r_k, b_ln_g, b_ln_b, b_v0, b_v1, b_v2, c_lb, c_norm, w_out, ffn_norm, w_up, w_down, final_norm):
    bsz, seq, d = x.shape
    depth = w_in.shape[0]
    l_real = N_META + seq
    lp = -(-l_real // STEP) * STEP
    while (bsz * lp) % ROW_TILE:
        lp += STEP
    h = jnp.concatenate([jnp.broadcast_to(meta[None].astype(x.dtype), (bsz, N_META, d)), x,
                         jnp.zeros((bsz, lp - l_real, d), x.dtype)], axis=1)
    h = h.reshape(bsz * lp, d)

    lb_all = jnp.cumsum(jax.nn.softmax(c_lb.astype(F32), axis=0), axis=0)
    lb_all = lb_all - lb_all[0]
    consts = _step_constants()
    row = lambda a: a.reshape(a.shape[0], 1, -1)
    wri = jnp.concatenate([_block_diag(a_w_r), _block_diag(a_w_i)], axis=2).astype(BF16)
    bri = row(jnp.concatenate([a_b_r, a_b_i], axis=1))
    wl = jnp.zeros((depth, B_LORA, 3 * B_WIDTH), F32)
    wl = wl.at[:, 0:32, 0:B_WIDTH].set(b_w2)
    wl = wl.at[:, 32:64, B_WIDTH:2 * B_WIDTH].set(b_a2)
    wl = wl.at[:, 64:128, 2 * B_WIDTH:].set(b_g2).astype(BF16)
    shared = [a_conv_w, row(a_conv_b), wri, bri, row(a_lambda), row(b_mu), wl, row(b_w0), row(b_a0),
              row(b_k_k), row(b_k_a), row(b_r_k), row(b_ln_g), row(b_ln_b)]
    vmix = [row(b_v0), b_v1.astype(BF16), b_v2.astype(BF16)]
    tail = [row(lb_all), row(jnp.tile(c_norm, (1, C_HEADS)))]
    g_mix, g_ffn = row(mix_norm), row(ffn_norm)
    w_up_b, w_down_b = w_up.astype(BF16), w_down.astype(BF16)

    v_first = None
    for l in range(depth):
        proj = _proj(h, g_mix, w_in, l).reshape(bsz, lp, N_IN)
        params = [(a, l) for a in shared] + ([(a, l - 1) for a in vmix] if l else []) + [(a, l) for a in tail]
        y, v_first = _mixer(proj, v_first, params, consts, l == 0)
        h = _post(h, y.reshape(bsz * lp, D_MIX), w_out, g_ffn, w_up_b, w_down_b,
                  final_norm.reshape(1, -1), l, l == depth - 1)
    return h.reshape(bsz, lp, d)[:, N_META:l_real]
```

```python
import functools

import numpy as np
import jax
import jax.numpy as jnp
from jax import lax
from jax.experimental import pallas as pl
from jax.experimental.pallas import tpu as pltpu

F32 = jnp.float32
BF16 = jnp.bfloat16

N_META = 16
NORM_EPS = 1e-6
CHUNK = 64
SUB = 3
NBATCH = 2
STEP = SUB * CHUNK

A_WIDTH = 384
A_HEADS = 6
A_C = 8.0
B_WIDTH = 384
B_HEAD = 64
B_HEADS = 6
B_LORA = 128
B_COLS = 3 * B_WIDTH + B_LORA
B_GN_EPS = 64e-5
C_WIDTH = 256
C_HEADS = 4
C_HEAD = 64
C_LEVELS = 6
PAIR = 128
STAGGER = 4
A_OFF = 0
B_OFF = 2 * A_WIDTH
C_OFF = B_OFF + B_COLS
N_IN = C_OFF + 4 * C_WIDTH
D_MIX = A_WIDTH + B_WIDTH + C_WIDTH

V7X_VMEM_LIMIT = 56 * 1024 * 1024
ROW_TILE = 704


def _dot(a, b, ca=1, cb=0):
    dims = (((ca,), (cb,)), ((), ()))
    return lax.dot_general(a.astype(BF16), b.astype(BF16), dims, preferred_element_type=F32)


def _split2(x):
    hi = x.astype(BF16)
    return hi, (x - hi.astype(F32)).astype(BF16)


def _sel_left(m, x):
    n = x.shape[1]
    r = jnp.dot(m, jnp.concatenate(_split2(x), axis=1), preferred_element_type=F32)
    return r[:, :n] + r[:, n:]


def _bd(y, first):
    return jnp.concatenate([jnp.where(first, y, 0.0), jnp.where(first, 0.0, y)], axis=0)


def _head_sums(x, bo):
    npair = x.shape[1] // PAIR
    rows = x.shape[0]
    r = _dot(jnp.concatenate([x[:, p * PAIR:(p + 1) * PAIR] for p in range(npair)], axis=0), bo)
    return jnp.concatenate([r[p * rows:(p + 1) * rows] for p in range(npair)], axis=1)


def _sigmoid(x):
    return jax.nn.sigmoid(x)


def _softplus(x):
    return jnp.maximum(x, 0.0) + jnp.log1p(jnp.exp(-jnp.abs(x)))


def _rmsnorm(x, g):
    return x * lax.rsqrt(jnp.mean(x * x, axis=-1, keepdims=True) + NORM_EPS) * g


def _proj_kernel(h_ref, g_ref, w_ref, o_ref, wb):
    @pl.when(pl.program_id(0) == 0)
    def _():
        wb[...] = w_ref[...].astype(BF16)

    xn = _rmsnorm(h_ref[...], g_ref[...])
    o_ref[...] = jnp.dot(xn.astype(BF16), wb[...], preferred_element_type=F32)


def _proj_first_kernel(l_real, x_ref, xp_ref, meta_ref, g_ref, w_ref, o_ref, h_ref, wb):
    j = pl.program_id(1)

    @pl.when((pl.program_id(0) == 0) & (j == 0))
    def _():
        wb[...] = w_ref[...].astype(BF16)

    top = jnp.where(j == 0, meta_ref[...], xp_ref[...])
    h = jnp.concatenate([top, x_ref[0:ROW_TILE - N_META, :]], axis=0)
    t = j * ROW_TILE + lax.broadcasted_iota(jnp.int32, h.shape, 0)
    h = jnp.where(t < l_real, h, 0.0)
    h_ref[...] = h
    xn = _rmsnorm(h, g_ref[...])
    o_ref[...] = jnp.dot(xn.astype(BF16), wb[...], preferred_element_type=F32)


def _proj_first(x, meta, g, w, lp):
    bsz, seq, d = x.shape
    n = w.shape[2]
    t = ROW_TILE
    per = t // N_META
    last_main = -(-seq // t) - 1
    last_prev = seq // N_META - 1
    assert lp // t == last_main + 1, "every tile of the padded length must overlap x"
    return pl.pallas_call(
        functools.partial(_proj_first_kernel, N_META + seq),
        grid=(bsz, lp // t),
        in_specs=[pl.BlockSpec((None, t, d), lambda b, j: (b, jnp.minimum(j, last_main), 0)),
                  pl.BlockSpec((None, N_META, d),
                               lambda b, j: (b, jnp.clip(j * per - 1, 0, last_prev), 0)),
                  pl.BlockSpec(meta.shape, lambda b, j: (0, 0)),
                  _layer_spec(g, 0), _layer_spec(w, 0, single=True)],
        out_specs=[pl.BlockSpec((None, t, n), lambda b, j: (b, j, 0)),
                   pl.BlockSpec((None, t, d), lambda b, j: (b, j, 0))],
        out_shape=[jax.ShapeDtypeStruct((bsz, lp, n), F32), jax.ShapeDtypeStruct((bsz, lp, d), F32)],
        scratch_shapes=[pltpu.VMEM((d, n), BF16)],
        compiler_params=pltpu.CompilerParams(dimension_semantics=("arbitrary", "arbitrary"),
                                             vmem_limit_bytes=V7X_VMEM_LIMIT),
        name="proj_first",
    )(x, x, meta, g, w)


def _layer_spec(arr, layer, single=False):
    zeros = (0,) * (arr.ndim - 1)
    kw = dict(pipeline_mode=pl.Buffered(1)) if single else {}
    return pl.BlockSpec((None,) + arr.shape[1:], lambda *_: (layer,) + zeros, **kw)


def _proj(h2d, g, w, layer):
    rows, d = h2d.shape
    n = w.shape[2]
    return pl.pallas_call(
        _proj_kernel,
        grid=(rows // ROW_TILE,),
        in_specs=[pl.BlockSpec((ROW_TILE, d), lambda i: (i, 0)),
                  _layer_spec(g, layer), _layer_spec(w, layer, single=True)],
        out_specs=pl.BlockSpec((ROW_TILE, n), lambda i: (i, 0)),
        out_shape=jax.ShapeDtypeStruct((rows, n), F32),
        scratch_shapes=[pltpu.VMEM((d, n), BF16)],
        compiler_params=pltpu.CompilerParams(dimension_semantics=("arbitrary",),
                                             vmem_limit_bytes=V7X_VMEM_LIMIT),
        name="proj",
    )(h2d, g, w)


def _residual_mlp(hs, ys, wob, g_ref, wu_ref, wd_ref, f_chunk):
    h1 = hs + jnp.dot(ys.astype(BF16), wob[...], preferred_element_type=F32)
    hb = _rmsnorm(h1, g_ref[...]).astype(BF16)
    acc = h1
    d_ff = wu_ref.shape[1]
    for c in range(d_ff // f_chunk):
        u = jnp.dot(hb, wu_ref[:, c * f_chunk:(c + 1) * f_chunk], preferred_element_type=F32)
        a = jnp.square(jnp.maximum(u, 0.0)).astype(BF16)
        acc = acc + jnp.dot(a, wd_ref[c * f_chunk:(c + 1) * f_chunk, :],
                            preferred_element_type=F32)
    return acc


def _post_kernel(f_chunk, h_ref, y_ref, wo_ref, g_ref, wu_ref, wd_ref, o_ref, wob):
    @pl.when(pl.program_id(0) == 0)
    def _():
        wob[...] = wo_ref[...].astype(BF16)

    o_ref[...] = _residual_mlp(h_ref[...], y_ref[...], wob, g_ref, wu_ref, wd_ref, f_chunk)


def _post(h2d, y2d, wo, g, wu, wd, layer):
    rows, d = h2d.shape
    tile = ROW_TILE
    return pl.pallas_call(
        functools.partial(_post_kernel, 1024),
        grid=(rows // tile,),
        in_specs=[pl.BlockSpec((tile, d), lambda i: (i, 0)),
                  pl.BlockSpec((tile, D_MIX), lambda i: (i, 0)),
                  _layer_spec(wo, layer, single=True), _layer_spec(g, layer),
                  _layer_spec(wu, layer, single=True), _layer_spec(wd, layer, single=True)],
        out_specs=pl.BlockSpec((tile, d), lambda i: (i, 0)),
        out_shape=jax.ShapeDtypeStruct((rows, d), F32),
        scratch_shapes=[pltpu.VMEM((D_MIX, d), BF16)],
        compiler_params=pltpu.CompilerParams(dimension_semantics=("arbitrary",),
                                             vmem_limit_bytes=V7X_VMEM_LIMIT),
        name="post",
    )(h2d, y2d, wo, g, wu, wd)


def _post_last_kernel(f_chunk, h_ref, hx_ref, y_ref, yx_ref, wo_ref, g_ref, wu_ref, wd_ref, gf_ref, o_ref, wob):
    @pl.when((pl.program_id(0) == 0) & (pl.program_id(1) == 0))
    def _():
        wob[...] = wo_ref[...].astype(BF16)

    shift = lambda main, extra: jnp.concatenate([main[N_META:, :], extra[...]], axis=0)
    acc = _residual_mlp(shift(h_ref, hx_ref), shift(y_ref, yx_ref), wob, g_ref, wu_ref, wd_ref, f_chunk)
    o_ref[...] = _rmsnorm(acc, gf_ref[...])


def _post_last(h3d, y3d, wo, g, wu, wd, gf, layer, seq):
    bsz, lp, d = h3d.shape
    t = ROW_TILE
    per = t // N_META
    last_extra = lp // N_META - 1
    assert -(-seq // t) == lp // t, "every tile of the padded length must overlap the output"
    main = lambda b, j: (b, j, 0)
    extra = lambda b, j: (b, jnp.minimum((j + 1) * per, last_extra), 0)
    return pl.pallas_call(
        functools.partial(_post_last_kernel, 1024),
        grid=(bsz, lp // t),
        in_specs=[pl.BlockSpec((None, t, d), main), pl.BlockSpec((None, N_META, d), extra),
                  pl.BlockSpec((None, t, D_MIX), main), pl.BlockSpec((None, N_META, D_MIX), extra),
                  _layer_spec(wo, layer, single=True), _layer_spec(g, layer),
                  _layer_spec(wu, layer, single=True), _layer_spec(wd, layer, single=True),
                  pl.BlockSpec(gf.shape, lambda b, j: (0, 0))],
        out_specs=pl.BlockSpec((None, t, d), main),
        out_shape=jax.ShapeDtypeStruct((bsz, seq, d), F32),
        scratch_shapes=[pltpu.VMEM((D_MIX, d), BF16)],
        compiler_params=pltpu.CompilerParams(dimension_semantics=("arbitrary", "arbitrary"),
                                             vmem_limit_bytes=V7X_VMEM_LIMIT),
        name="post_last",
    )(h3d, h3d, y3d, y3d, wo, g, wu, wd, gf)


def _rows(j):
    return slice(j * CHUNK, (j + 1) * CHUNK)


def _last_rows(x):
    return jnp.concatenate([jnp.broadcast_to(x[(j + 1) * CHUNK - 1:(j + 1) * CHUNK, :], (CHUNK, x.shape[1]))
                            for j in range(SUB)], axis=0)


def _level_decays(b):
    rows, n = b.shape
    out = []
    sub = lax.broadcasted_iota(jnp.int32, (rows, n), 0) & 7
    b3 = b.reshape(rows // 8, 8, n)
    pick = lambda k: jnp.broadcast_to(b3[:, k:k + 1, :], b3.shape).reshape(rows, n)
    for l in range(C_LEVELS):
        half = 1 << l
        if l == 0:
            m = jnp.where((sub & 1) == 1, pltpu.roll(b, 1, 0), b)
        elif l == 1:
            m = jnp.where(sub < 4, pick(1), pick(5))
        elif l == 2:
            m = pick(3)
        else:
            blk = 2 * half
            m = jnp.concatenate([jnp.broadcast_to(b[i * blk + half - 1:i * blk + half, :], (blk, n))
                                 for i in range(rows // blk)], axis=0)
        out.append(-jnp.abs(b - m))
    return out


def _group_a(bi, p_ref, o_ref, cw_ref, cb_ref, wri_ref, bri_ref, lam_ref, xbuf, h_a):
    t = STEP
    xa = p_ref[bi, :, A_OFF:A_OFF + A_WIDTH]
    ga = p_ref[bi, :, A_OFF + A_WIDTH:A_OFF + 2 * A_WIDTH]
    xbuf[bi, 8:8 + t, :] = xa
    u = cb_ref[...]
    for j in range(4):
        u = u + cw_ref[j:j + 1, :] * xbuf[bi, 5 + j:5 + j + t, :]
    xbuf[bi, 0:8, :] = xbuf[bi, t:t + 8, :]
    yield
    ri = [_dot(u[:, p * PAIR:(p + 1) * PAIR], wri_ref[p]) for p in range(A_WIDTH // PAIR)]
    r = _sigmoid(jnp.concatenate([x[:, :PAIR] for x in ri], axis=1) + bri_ref[:, :A_WIDTH])
    i = _sigmoid(jnp.concatenate([x[:, PAIR:] for x in ri], axis=1) + bri_ref[:, A_WIDTH:])
    yield
    log_a = (-A_C * r) * _softplus(-lam_ref[...])
    a = jnp.exp(log_a)
    om = 1.0 - a * a
    d = (om * lax.rsqrt(jnp.maximum(om, 1e-30))) * (i * u)
    yield
    rows = lax.broadcasted_iota(jnp.int32, (t, A_WIDTH), 0) & 7
    for s in (1, 2, 4):
        a_sh = jnp.where(rows < s, 1.0, pltpu.roll(a, s, 0))
        d_sh = jnp.where(rows < s, 0.0, pltpu.roll(d, s, 0))
        d = a * d_sh + d
        a = a * a_sh
        yield
    carry = h_a[bi, 0:1, :]
    hs = []
    for g in range(t // 8):
        hg = d[8 * g:8 * g + 8, :] + a[8 * g:8 * g + 8, :] * carry
        carry = hg[7:8, :]
        hs.append(hg)
        if g % 8 == 7:
            yield
    hseq = jnp.concatenate(hs, axis=0)
    h_a[bi, 0:1, :] = carry
    o_ref[bi, :, 0:A_WIDTH] = hseq * jax.nn.gelu(ga, approximate=True)


def _group_c(bi, p_ref, o_ref, lb_ref, cn_ref, tril, bo, bdm, s_c):
    t = CHUNK
    q_raw = p_ref[bi, :, C_OFF:C_OFF + C_WIDTH]
    fz = p_ref[bi, :, C_OFF + C_WIDTH:C_OFF + 2 * C_WIDTH]
    v = p_ref[bi, :, C_OFF + 2 * C_WIDTH:C_OFF + 3 * C_WIDTH]
    og = p_ref[bi, :, C_OFF + 3 * C_WIDTH:C_OFF + 4 * C_WIDTH]
    lb = lb_ref[...]
    f = lb + (1.0 - lb) * _sigmoid(fz)
    g = jnp.log(f)
    kd = 1.0 - f
    q = q_raw * _sigmoid(q_raw)
    yield
    b = _sel_left(tril, g)
    e_lev = [jnp.exp(x) for x in _level_decays(b)]
    b_last = _last_rows(b)
    qe = q * jnp.exp(b)
    kl = kd * jnp.exp(b_last - b)
    g_last = jnp.exp(b_last)
    dq = _head_sums(q * kd, bo)
    yield
    ti = lax.broadcasted_iota(jnp.int32, (t, PAIR), 0)
    lane = lax.broadcasted_iota(jnp.int32, (t, PAIR), 1)
    first = lane < C_HEAD
    sj = lane & (C_HEAD - 1)
    xs = jnp.where(ti > sj, ti ^ sj, 0)
    npair = C_WIDTH // PAIR
    cs = [(j, p) for j in range(SUB) for p in range(npair)]
    sl = lambda x, c: x[_rows(c[0]), c[1] * PAIR:(c[1] + 1) * PAIR]
    att = [jnp.zeros((t, PAIR), F32) for _ in cs]
    for l in range(C_LEVELS):
        el = e_lev[l]
        ql = q * el
        kl_l = kd * el
        lev = (xs >> l) == 1
        att = [att[i] + jnp.where(lev, _dot(sl(ql, c), _bd(sl(kl_l, c), first), 1, 1), 0.0)
               for i, c in enumerate(cs)]
        yield
    intra = [_dot(att[i], _bd(sl(v, c), first)) + sl(dq, c) * sl(v, c) for i, c in enumerate(cs)]
    upd = [jnp.where(bdm, _dot(sl(v, c), sl(kl, c), 0, 0), 0.0) for c in cs]
    yield
    sts = {}
    for p in range(npair):
        st = s_c[bi, p]
        for j in range(SUB):
            i = cs.index((j, p))
            sts[i] = st
            st = st * g_last[(j + 1) * t - 1:(j + 1) * t, p * PAIR:(p + 1) * PAIR] + upd[i]
        s_c[bi, p] = st
    outs = [intra[i] + _dot(sl(qe, c), sts[i], 1, 1) for i, c in enumerate(cs)]
    o = jnp.concatenate([jnp.concatenate([outs[cs.index((j, p))] for p in range(npair)], axis=1)
                         for j in range(SUB)], axis=0)
    yield
    ms = _head_sums(o * o, bo) * (1.0 / C_HEAD)
    o_ref[bi, :, A_WIDTH + B_WIDTH:D_MIX] = o * lax.rsqrt(ms + NORM_EPS) * cn_ref[...] * (og * _sigmoid(og))


def _group_b(bi, p_ref, o_ref, vf_ref, vfo_ref, mu_ref, wl_ref, w0_ref, a0_ref, kk_ref, ka_ref, rk_ref,
             lng_ref, lnb_ref, v0_ref, v1_ref, v2_ref, tril, bo, bdm, pbuf, s_b):
    t = CHUNK
    w = B_WIDTH
    pb = p_ref[bi, :, B_OFF:B_OFF + B_COLS]
    pbuf[bi, 8:8 + STEP, :] = pb
    prev = pbuf[bi, 7:7 + STEP, :]
    pbuf[bi, 0:8, :] = pbuf[bi, STEP:STEP + 8, :]
    sh = pb + mu_ref[...] * (prev - pb)
    r = sh[:, 0:w]
    k = sh[:, w:2 * w]
    v = sh[:, 2 * w:3 * w]
    lo = sh[:, 3 * w:3 * w + B_LORA]
    lane = lax.broadcasted_iota(jnp.int32, (STEP, B_LORA), 1)
    xl = jnp.where(lane < 32, jnp.tanh(lo), jnp.where(lane < 64, lo, _sigmoid(lo)))
    lora = _dot(xl, wl_ref[...])
    lw = -float(np.exp(-0.5)) * _sigmoid(w0_ref[...] + lora[:, 0:w])
    if vf_ref is None:
        vfo_ref[bi] = v
    else:
        mix = _sigmoid(v0_ref[...] + _dot(_dot(v, v1_ref[...]), v2_ref[...]))
        v = v + (vf_ref[bi] - v) * mix
    aa = _sigmoid(a0_ref[...] + lora[:, w:2 * w])
    gate = lora[:, 2 * w:3 * w]
    kk = k * kk_ref[...]
    k = k * (1.0 + (aa - 1.0) * ka_ref[...])
    sums = _head_sums(jnp.concatenate([kk * kk, r * k * rk_ref[...]], axis=0), bo)
    kk = kk * lax.rsqrt(jnp.maximum(sums[:STEP], 1e-24))
    rk = sums[STEP:]
    ap = -kk
    bb = kk * aa
    c = _sel_left(tril, lw)
    c_last = _last_rows(c)
    a_h = ap * jnp.exp(c - lw)
    r_h = r * jnp.exp(c)
    e_inv = jnp.exp(-c)
    b_h = bb * e_inv
    k_h = k * e_inv
    e_cl = jnp.exp(c_last - c)
    b_c = bb * e_cl
    k_c = k * e_cl
    gam = jnp.exp(c_last)
    yield

    ti = lax.broadcasted_iota(jnp.int32, (t, PAIR), 0)
    lane = lax.broadcasted_iota(jnp.int32, (t, PAIR), 1)
    first = lane < B_HEAD
    sj = lane & (B_HEAD - 1)
    strict = ti > sj
    lower = ti >= sj
    blk = (ti >> 4) == (sj >> 4)
    eye = jnp.where(ti == sj, 1.0, 0.0)
    bd = lambda y: _bd(y, first)
    cat0 = lambda xs: jnp.concatenate(xs, axis=0)
    cat1 = lambda xs: jnp.concatenate(xs, axis=1)
    npair = B_WIDTH // PAIR
    cs = [(j, p) for j in range(SUB) for p in range(npair)]
    n = range(len(cs))
    sl = lambda x, i: x[_rows(cs[i][0]), cs[i][1] * PAIR:(cs[i][1] + 1) * PAIR]
    vbd = [bd(sl(v, i)) for i in n]
    ar = [cat0([sl(a_h, i), sl(r_h, i)]) for i in n]
    mm = [_dot(ar[i], cat0([bd(sl(b_h, i)), bd(sl(k_h, i))]), 1, 1) for i in n]
    yield
    m_ab = [jnp.where(strict, m[:t, :PAIR], 0.0) for m in mm]
    m_ak = [jnp.where(strict, m[:t, PAIR:], 0.0) for m in mm]
    n_rb = [jnp.where(lower, m[t:, :PAIR], 0.0) for m in mm]
    n_rk = [jnp.where(lower, m[t:, PAIR:], 0.0) for m in mm]
    dg = [jnp.where(blk, m, 0.0) for m in m_ab]
    off = [m_ab[i] - dg[i] for i in n]
    akv = [_dot(m_ak[i], vbd[i]) for i in n]
    nkv = [_dot(n_rk[i], vbd[i]) for i in n]
    d2 = [_dot(dg[i], bd(dg[i])) for i in n]
    yield
    s4 = [_dot(cat0([d2[i], dg[i]]), bd(d2[i])) for i in n]
    yield
    d4 = [x[:t] for x in s4]
    x1 = [eye + dg[i] + d2[i] + s4[i][t:] for i in n]
    s8 = [_dot(cat0([d4[i], x1[i]]), bd(d4[i])) for i in n]
    yield
    x2 = [x1[i] + s8[i][t:] for i in n]
    dinv = [x2[i] + _dot(x2[i], bd(s8[i][:t])) for i in n]
    yield
    gm = [_dot(dinv[i], bd(off[i])) for i in n]
    yield
    g2 = [_dot(gm[i], bd(gm[i])) for i in n]
    yield
    gs = [eye + gm[i] + g2[i] + _dot(gm[i], bd(g2[i])) for i in n]
    yield
    tinv = [_dot(gs[i], bd(dinv[i])) for i in n]
    yield
    pq = [_dot(tinv[i], cat1([bd(sl(a_h, i)), bd(akv[i])])) for i in n]
    yield
    npq = [_dot(n_rb[i], cat1([bd(pq[i][:, :PAIR]), bd(pq[i][:, PAIR:])])) for i in n]
    yield
    p2 = [sl(r_h, i) + npq[i][:, :PAIR] for i in n]
    q2 = [npq[i][:, PAIR:] + nkv[i] for i in n]
    wp = [jnp.where(bdm, _dot(pq[i][:, :PAIR], sl(b_c, i), 0, 0), 0.0) for i in n]
    zz = [jnp.where(bdm, _dot(cat0([pq[i][:, PAIR:], sl(v, i)]), cat0([sl(b_c, i), sl(k_c, i)]), 0, 0), 0.0)
          for i in n]
    yield
    sts = {}
    for p in range(npair):
        st = s_b[bi, p]
        for j in range(SUB):
            i = cs.index((j, p))
            sts[i] = st
            st = st * gam[(j + 1) * t - 1:(j + 1) * t, p * PAIR:(p + 1) * PAIR] + _dot(st, wp[i]) + zz[i]
        s_b[bi, p] = st
    ys = [_dot(p2[i], sts[i], 1, 1) + q2[i] for i in n]
    yield
    y = cat0([cat1([ys[cs.index((j, p))] for p in range(npair)]) for j in range(SUB)])
    inv_n = 1.0 / B_HEAD
    mean = _head_sums(y, bo) * inv_n
    yc = y - mean
    var = _head_sums(yc * yc, bo) * inv_n
    y = yc * lax.rsqrt(var + B_GN_EPS) * lng_ref[...] + lnb_ref[...]
    y = y + rk * v
    o_ref[bi, :, A_WIDTH:A_WIDTH + B_WIDTH] = y * gate


def _mixer_kernel(layer0, *refs):
    it = iter(refs)
    p_ref = next(it)
    vf_ref = None if layer0 else next(it)
    cw, cb, wri, bri, lam = (next(it) for _ in range(5))
    mu, wl, w0, a0, kkp, kap, rkp, lng, lnb = (next(it) for _ in range(9))
    v0 = v1 = v2 = None
    if not layer0:
        v0, v1, v2 = (next(it) for _ in range(3))
    lb, cn, tril_ref, bo_ref = (next(it) for _ in range(4))
    y_ref = next(it)
    vfo_ref = next(it) if layer0 else None
    xbuf, h_a, pbuf, s_b, s_c = (next(it) for _ in range(5))

    @pl.when(pl.program_id(1) == 0)
    def _():
        xbuf[:, 0:8, :] = jnp.zeros((NBATCH, 8, A_WIDTH), F32)
        pbuf[:, 0:8, :] = jnp.zeros((NBATCH, 8, B_COLS), F32)
        h_a[...] = jnp.zeros_like(h_a)
        s_b[...] = jnp.zeros_like(s_b)
        s_c[...] = jnp.zeros_like(s_c)

    tril = tril_ref[...]
    bo = bo_ref[...]
    half = lambda d: lax.broadcasted_iota(jnp.int32, (PAIR, PAIR), d) < PAIR // 2
    bdm = half(0) == half(1)
    live = []
    for bi in range(NBATCH):
        gb = _group_b(bi, p_ref, y_ref, vf_ref, vfo_ref, mu, wl, w0, a0, kkp, kap, rkp, lng, lnb, v0, v1, v2,
                      tril, bo, bdm, pbuf, s_b)
        gc = _group_c(bi, p_ref, y_ref, lb, cn, tril, bo, bdm, s_c)
        ga = _group_a(bi, p_ref, y_ref, cw, cb, wri, bri, lam, xbuf, h_a)
        next(gb)
        live += [gb, gc, ga]
        rounds = STAGGER if bi + 1 < NBATCH else 1 << 30
        while live and rounds:
            rounds -= 1
            for gen in list(live):
                try:
                    next(gen)
                except StopIteration:
                    live.remove(gen)


def _step_constants():
    eye = np.eye(SUB, dtype=np.float32)
    tril = np.kron(eye, np.tril(np.ones((CHUNK, CHUNK), np.float32)))
    hid = np.arange(PAIR) // B_HEAD
    bo = (hid[:, None] == hid[None, :]).astype(np.float32)
    return tuple(jnp.asarray(m, BF16) for m in (tril, bo))


def _mixer(proj, v_first, params, consts, layer0):
    bsz, lp, _ = proj.shape
    t = STEP
    const = lambda b, c: (0, 0)
    row = lambda b, c: (b, c, 0)
    ins = [proj]
    specs = [pl.BlockSpec((NBATCH, t, N_IN), row)]
    if not layer0:
        ins.append(v_first)
        specs.append(pl.BlockSpec((NBATCH, t, B_WIDTH), row))
    for arr, layer in params:
        ins.append(arr)
        specs.append(_layer_spec(arr, layer))
    for arr in consts:
        ins.append(arr)
        specs.append(pl.BlockSpec(arr.shape, const))
    out_shape = [jax.ShapeDtypeStruct((bsz, lp, D_MIX), F32)]
    out_specs = [pl.BlockSpec((NBATCH, t, D_MIX), row)]
    if layer0:
        out_shape.append(jax.ShapeDtypeStruct((bsz, lp, B_WIDTH), F32))
        out_specs.append(pl.BlockSpec((NBATCH, t, B_WIDTH), row))
    res = pl.pallas_call(
        functools.partial(_mixer_kernel, layer0),
        grid=(bsz // NBATCH, lp // t),
        in_specs=specs,
        out_specs=out_specs,
        out_shape=out_shape,
        scratch_shapes=[pltpu.VMEM((NBATCH, t + 8, A_WIDTH), F32),
                        pltpu.VMEM((NBATCH, 8, A_WIDTH), F32),
                        pltpu.VMEM((NBATCH, t + 8, B_COLS), F32),
                        pltpu.VMEM((NBATCH, B_WIDTH // PAIR, PAIR, PAIR), F32),
                        pltpu.VMEM((NBATCH, C_WIDTH // PAIR, PAIR, PAIR), F32)],
        compiler_params=pltpu.CompilerParams(dimension_semantics=("arbitrary", "arbitrary"),
                                             vmem_limit_bytes=V7X_VMEM_LIMIT),
        name="mixer0" if layer0 else "mixer",
    )(*ins)
    return res if layer0 else (res[0], v_first)


def _block_diag(w):
    nl, hn, n, _ = w.shape
    out = jnp.zeros((nl, hn * n, hn * n), w.dtype)
    for h in range(hn):
        out = out.at[:, h * n:(h + 1) * n, h * n:(h + 1) * n].set(w[:, h])
    return out


def kernel(x, meta, mix_norm, w_in, a_conv_w, a_conv_b, a_w_r, a_b_r, a_w_i, a_b_i, a_lambda, b_mu, b_w0, b_w2, b_a0, b_a2, b_g2, b_k_k, b_k_a, b_r_k, b_ln_g, b_ln_b, b_v0, b_v1, b_v2, c_lb, c_norm, w_out, ffn_norm, w_up, w_down, final_norm):
    bsz, seq, d = x.shape
    depth = w_in.shape[0]
    l_real = N_META + seq
    lp = -(-l_real // STEP) * STEP
    while (bsz * lp) % ROW_TILE:
        lp += STEP
    lb_all = jnp.cumsum(jax.nn.softmax(c_lb.astype(F32), axis=0), axis=0)
    lb_all = lb_all - lb_all[0]
    consts = _step_constants()
    row = lambda a: a.reshape(a.shape[0], 1, -1)
    pair_bd = lambda w: _block_diag(w.reshape(depth * (A_HEADS // 2), 2, *w.shape[2:]))
    wri = jnp.concatenate([pair_bd(a_w_r), pair_bd(a_w_i)], axis=2).astype(BF16)
    wri = wri.reshape(depth, A_HEADS // 2, PAIR, 2 * PAIR)
    bri = row(jnp.concatenate([a_b_r, a_b_i], axis=1))
    wl = jnp.zeros((depth, B_LORA, 3 * B_WIDTH), F32)
    wl = wl.at[:, 0:32, 0:B_WIDTH].set(b_w2)
    wl = wl.at[:, 32:64, B_WIDTH:2 * B_WIDTH].set(b_a2)
    wl = wl.at[:, 64:128, 2 * B_WIDTH:].set(b_g2).astype(BF16)
    shared = [a_conv_w, row(a_conv_b), wri, bri, row(a_lambda), row(b_mu), wl, row(b_w0), row(b_a0),
              row(b_k_k), row(b_k_a), row(b_r_k), row(b_ln_g), row(b_ln_b)]
    vmix = [row(b_v0), b_v1.astype(BF16), b_v2.astype(BF16)]
    tail = [row(lb_all), row(jnp.tile(c_norm, (1, C_HEADS)))]
    g_mix, g_ffn = row(mix_norm), row(ffn_norm)
    w_up_b, w_down_b = w_up.astype(BF16), w_down.astype(BF16)

    v_first = None
    for l in range(depth):
        if l == 0:
            proj, h = _proj_first(x, meta.astype(x.dtype), g_mix, w_in, lp)
            h = h.reshape(bsz * lp, d)
        else:
            proj = _proj(h, g_mix, w_in, l).reshape(bsz, lp, N_IN)
        params = [(a, l) for a in shared] + ([(a, l - 1) for a in vmix] if l else []) + [(a, l) for a in tail]
        y, v_first = _mixer(proj, v_first, params, consts, l == 0)
        if l < depth - 1:
            h = _post(h, y.reshape(bsz * lp, D_MIX), w_out, g_ffn, w_up_b, w_down_b, l)
    return _post_last(h.reshape(bsz, lp, d), y, w_out, g_ffn, w_up_b, w_down_b,
                      final_norm.reshape(1, -1), depth - 1, seq)
```

```python
import functools

import numpy as np
import jax
import jax.numpy as jnp
from jax import lax
from jax.experimental import pallas as pl
from jax.experimental.pallas import tpu as pltpu

F32 = jnp.float32
BF16 = jnp.bfloat16

N_META = 16
NORM_EPS = 1e-6
CHUNK = 64
SUB = 3
NBATCH = 4
STEP = SUB * CHUNK

A_WIDTH = 384
A_HEADS = 6
A_C = 8.0
B_WIDTH = 384
B_HEAD = 64
B_HEADS = 6
B_LORA = 128
B_COLS = 3 * B_WIDTH + B_LORA
B_GN_EPS = 64e-5
C_WIDTH = 256
C_HEADS = 4
C_HEAD = 64
C_LEVELS = 6
PAIR = 128
STAGGER = 4
A_OFF = 0
B_OFF = 2 * A_WIDTH
C_OFF = B_OFF + B_COLS
N_IN = C_OFF + 4 * C_WIDTH
D_MIX = A_WIDTH + B_WIDTH + C_WIDTH

V7X_VMEM_LIMIT = 56 * 1024 * 1024
F_CHUNK = 1024
TINY = 1e-30
KK_NORM_FLOOR = 1e-12
ROW_TILE = 704


def _dot(a, b, ca=1, cb=0):
    dims = (((ca,), (cb,)), ((), ()))
    return lax.dot_general(a.astype(BF16), b.astype(BF16), dims, preferred_element_type=F32)


def _split2(x):
    hi = x.astype(BF16)
    return hi, (x - hi.astype(F32)).astype(BF16)


def _sel_left(m, x):
    n = x.shape[1]
    r = jnp.dot(m, jnp.concatenate(_split2(x), axis=1), preferred_element_type=F32)
    return r[:, :n] + r[:, n:]


def _bd(y, first):
    return jnp.concatenate([jnp.where(first, y, 0.0), jnp.where(first, 0.0, y)], axis=0)


def _head_sums(x, bo):
    npair = x.shape[1] // PAIR
    rows = x.shape[0]
    r = _dot(jnp.concatenate([x[:, p * PAIR:(p + 1) * PAIR] for p in range(npair)], axis=0), bo)
    return jnp.concatenate([r[p * rows:(p + 1) * rows] for p in range(npair)], axis=1)


def _sigmoid(x):
    return jax.nn.sigmoid(x)


def _softplus(x):
    return jnp.maximum(x, 0.0) + jnp.log1p(jnp.exp(-jnp.abs(x)))


def _rmsnorm(x, g):
    return x * lax.rsqrt(jnp.mean(x * x, axis=-1, keepdims=True) + NORM_EPS) * g


def _proj_kernel(h_ref, g_ref, w_ref, o_ref, wb):
    @pl.when(pl.program_id(0) == 0)
    def _():
        wb[...] = w_ref[...].astype(BF16)

    xn = _rmsnorm(h_ref[...], g_ref[...])
    o_ref[...] = jnp.dot(xn.astype(BF16), wb[...], preferred_element_type=F32)


def _proj_first_kernel(l_real, x_ref, xp_ref, meta_ref, g_ref, w_ref, o_ref, h_ref, wb):
    j = pl.program_id(1)

    @pl.when((pl.program_id(0) == 0) & (j == 0))
    def _():
        wb[...] = w_ref[...].astype(BF16)

    top = jnp.where(j == 0, meta_ref[...], xp_ref[...])
    h = jnp.concatenate([top, x_ref[0:ROW_TILE - N_META, :]], axis=0)
    t = j * ROW_TILE + lax.broadcasted_iota(jnp.int32, h.shape, 0)
    h = jnp.where(t < l_real, h, 0.0)
    h_ref[...] = h
    xn = _rmsnorm(h, g_ref[...])
    o_ref[...] = jnp.dot(xn.astype(BF16), wb[...], preferred_element_type=F32)


def _proj_first(x, meta, g, w, lp):
    bsz, seq, d = x.shape
    n = w.shape[2]
    t = ROW_TILE
    per = t // N_META
    last_main = -(-seq // t) - 1
    last_prev = seq // N_META - 1
    assert lp // t == last_main + 1, "every tile of the padded length must overlap x"
    return pl.pallas_call(
        functools.partial(_proj_first_kernel, N_META + seq),
        grid=(bsz, lp // t),
        in_specs=[pl.BlockSpec((None, t, d), lambda b, j: (b, jnp.minimum(j, last_main), 0)),
                  pl.BlockSpec((None, N_META, d),
                               lambda b, j: (b, jnp.clip(j * per - 1, 0, last_prev), 0)),
                  pl.BlockSpec(meta.shape, lambda b, j: (0, 0)),
                  _layer_spec(g, 0), _layer_spec(w, 0, single=True)],
        out_specs=[pl.BlockSpec((None, t, n), lambda b, j: (b, j, 0)),
                   pl.BlockSpec((None, t, d), lambda b, j: (b, j, 0))],
        out_shape=[jax.ShapeDtypeStruct((bsz, lp, n), F32), jax.ShapeDtypeStruct((bsz, lp, d), F32)],
        scratch_shapes=[pltpu.VMEM((d, n), BF16)],
        compiler_params=pltpu.CompilerParams(dimension_semantics=("arbitrary", "arbitrary"),
                                             vmem_limit_bytes=V7X_VMEM_LIMIT),
        name="proj_first",
    )(x, x, meta, g, w)


def _layer_spec(arr, layer, single=False):
    zeros = (0,) * (arr.ndim - 1)
    kw = dict(pipeline_mode=pl.Buffered(1)) if single else {}
    return pl.BlockSpec((None,) + arr.shape[1:], lambda *_: (layer,) + zeros, **kw)


def _proj(h2d, g, w, layer):
    rows, d = h2d.shape
    n = w.shape[2]
    return pl.pallas_call(
        _proj_kernel,
        grid=(rows // ROW_TILE,),
        in_specs=[pl.BlockSpec((ROW_TILE, d), lambda i: (i, 0)),
                  _layer_spec(g, layer), _layer_spec(w, layer, single=True)],
        out_specs=pl.BlockSpec((ROW_TILE, n), lambda i: (i, 0)),
        out_shape=jax.ShapeDtypeStruct((rows, n), F32),
        scratch_shapes=[pltpu.VMEM((d, n), BF16)],
        compiler_params=pltpu.CompilerParams(dimension_semantics=("arbitrary",),
                                             vmem_limit_bytes=V7X_VMEM_LIMIT),
        name="proj",
    )(h2d, g, w)


def _residual_mlp(hs, ys, wob, g_ref, wu_ref, wd_ref, f_chunk):
    h1 = hs + jnp.dot(ys.astype(BF16), wob[...], preferred_element_type=F32)
    hb = _rmsnorm(h1, g_ref[...]).astype(BF16)
    acc = h1
    d_ff = wu_ref.shape[1]
    for c in range(d_ff // f_chunk):
        u = jnp.dot(hb, wu_ref[:, c * f_chunk:(c + 1) * f_chunk], preferred_element_type=F32)
        a = jnp.square(jnp.maximum(u, 0.0)).astype(BF16)
        acc = acc + jnp.dot(a, wd_ref[c * f_chunk:(c + 1) * f_chunk, :],
                            preferred_element_type=F32)
    return acc


def _post_kernel(f_chunk, h_ref, y_ref, wo_ref, g_ref, wu_ref, wd_ref, o_ref, wob):
    @pl.when(pl.program_id(0) == 0)
    def _():
        wob[...] = wo_ref[...].astype(BF16)

    o_ref[...] = _residual_mlp(h_ref[...], y_ref[...], wob, g_ref, wu_ref, wd_ref, f_chunk)


def _post(h2d, y2d, wo, g, wu, wd, layer):
    rows, d = h2d.shape
    tile = ROW_TILE
    return pl.pallas_call(
        functools.partial(_post_kernel, F_CHUNK),
        grid=(rows // tile,),
        in_specs=[pl.BlockSpec((tile, d), lambda i: (i, 0)),
                  pl.BlockSpec((tile, D_MIX), lambda i: (i, 0)),
                  _layer_spec(wo, layer, single=True), _layer_spec(g, layer),
                  _layer_spec(wu, layer, single=True), _layer_spec(wd, layer, single=True)],
        out_specs=pl.BlockSpec((tile, d), lambda i: (i, 0)),
        out_shape=jax.ShapeDtypeStruct((rows, d), F32),
        scratch_shapes=[pltpu.VMEM((D_MIX, d), BF16)],
        compiler_params=pltpu.CompilerParams(dimension_semantics=("arbitrary",),
                                             vmem_limit_bytes=V7X_VMEM_LIMIT),
        name="post",
    )(h2d, y2d, wo, g, wu, wd)


def _post_last_kernel(f_chunk, h_ref, hx_ref, y_ref, yx_ref, wo_ref, g_ref, wu_ref, wd_ref, gf_ref, o_ref, wob):
    @pl.when((pl.program_id(0) == 0) & (pl.program_id(1) == 0))
    def _():
        wob[...] = wo_ref[...].astype(BF16)

    shift = lambda main, extra: jnp.concatenate([main[N_META:, :], extra[...]], axis=0)
    acc = _residual_mlp(shift(h_ref, hx_ref), shift(y_ref, yx_ref), wob, g_ref, wu_ref, wd_ref, f_chunk)
    o_ref[...] = _rmsnorm(acc, gf_ref[...])


def _post_last(h3d, y3d, wo, g, wu, wd, gf, layer, seq):
    bsz, lp, d = h3d.shape
    t = ROW_TILE
    per = t // N_META
    last_extra = lp // N_META - 1
    assert -(-seq // t) == lp // t, "every tile of the padded length must overlap the output"
    main = lambda b, j: (b, j, 0)
    extra = lambda b, j: (b, jnp.minimum((j + 1) * per, last_extra), 0)
    return pl.pallas_call(
        functools.partial(_post_last_kernel, F_CHUNK),
        grid=(bsz, lp // t),
        in_specs=[pl.BlockSpec((None, t, d), main), pl.BlockSpec((None, N_META, d), extra),
                  pl.BlockSpec((None, t, D_MIX), main), pl.BlockSpec((None, N_META, D_MIX), extra),
                  _layer_spec(wo, layer, single=True), _layer_spec(g, layer),
                  _layer_spec(wu, layer, single=True), _layer_spec(wd, layer, single=True),
                  pl.BlockSpec(gf.shape, lambda b, j: (0, 0))],
        out_specs=pl.BlockSpec((None, t, d), main),
        out_shape=jax.ShapeDtypeStruct((bsz, seq, d), F32),
        scratch_shapes=[pltpu.VMEM((D_MIX, d), BF16)],
        compiler_params=pltpu.CompilerParams(dimension_semantics=("arbitrary", "arbitrary"),
                                             vmem_limit_bytes=V7X_VMEM_LIMIT),
        name="post_last",
    )(h3d, h3d, y3d, y3d, wo, g, wu, wd, gf)


def _rows(j):
    return slice(j * CHUNK, (j + 1) * CHUNK)


def _last_rows(x):
    return jnp.concatenate([jnp.broadcast_to(x[(j + 1) * CHUNK - 1:(j + 1) * CHUNK, :], (CHUNK, x.shape[1]))
                            for j in range(SUB)], axis=0)


def _level_decays(b):
    rows, n = b.shape
    out = []
    sub = lax.broadcasted_iota(jnp.int32, (rows, n), 0) & 7
    b3 = b.reshape(rows // 8, 8, n)
    pick = lambda k: jnp.broadcast_to(b3[:, k:k + 1, :], b3.shape).reshape(rows, n)
    for l in range(C_LEVELS):
        half = 1 << l
        if l == 0:
            m = jnp.where((sub & 1) == 1, pltpu.roll(b, 1, 0), b)
        elif l == 1:
            m = jnp.where(sub < 4, pick(1), pick(5))
        elif l == 2:
            m = pick(3)
        else:
            blk = 2 * half
            m = jnp.concatenate([jnp.broadcast_to(b[i * blk + half - 1:i * blk + half, :], (blk, n))
                                 for i in range(rows // blk)], axis=0)
        out.append(-jnp.abs(b - m))
    return out


def _group_a(bi, p_ref, o_ref, cw_ref, cb_ref, wri_ref, bri_ref, lam_ref, xbuf, h_a):
    t = STEP
    xa = p_ref[bi, :, A_OFF:A_OFF + A_WIDTH]
    ga = p_ref[bi, :, A_OFF + A_WIDTH:A_OFF + 2 * A_WIDTH]
    xbuf[bi, 8:8 + t, :] = xa
    u = cb_ref[...]
    for j in range(4):
        u = u + cw_ref[j:j + 1, :] * xbuf[bi, 5 + j:5 + j + t, :]
    xbuf[bi, 0:8, :] = xbuf[bi, t:t + 8, :]
    yield
    ri = [_dot(u[:, p * PAIR:(p + 1) * PAIR], wri_ref[p]) for p in range(A_WIDTH // PAIR)]
    r = _sigmoid(jnp.concatenate([x[:, :PAIR] for x in ri], axis=1) + bri_ref[:, :A_WIDTH])
    i = _sigmoid(jnp.concatenate([x[:, PAIR:] for x in ri], axis=1) + bri_ref[:, A_WIDTH:])
    yield
    log_a = (-A_C * r) * _softplus(-lam_ref[...])
    a = jnp.exp(log_a)
    om = 1.0 - a * a
    d = (om * lax.rsqrt(jnp.maximum(om, TINY))) * (i * u)
    yield
    rows = lax.broadcasted_iota(jnp.int32, (t, A_WIDTH), 0) & 7
    for s in (1, 2, 4):
        a_sh = jnp.where(rows < s, 1.0, pltpu.roll(a, s, 0))
        d_sh = jnp.where(rows < s, 0.0, pltpu.roll(d, s, 0))
        d = a * d_sh + d
        a = a * a_sh
        yield
    carry = h_a[bi, 0:1, :]
    hs = []
    for g in range(t // 8):
        hg = d[8 * g:8 * g + 8, :] + a[8 * g:8 * g + 8, :] * carry
        carry = hg[7:8, :]
        hs.append(hg)
        if g % 8 == 7:
            yield
    hseq = jnp.concatenate(hs, axis=0)
    h_a[bi, 0:1, :] = carry
    o_ref[bi, :, 0:A_WIDTH] = hseq * jax.nn.gelu(ga, approximate=True)


def _group_c(bi, p_ref, o_ref, lb_ref, cn_ref, tril, bo, bdm, s_c):
    t = CHUNK
    q_raw = p_ref[bi, :, C_OFF:C_OFF + C_WIDTH]
    fz = p_ref[bi, :, C_OFF + C_WIDTH:C_OFF + 2 * C_WIDTH]
    v = p_ref[bi, :, C_OFF + 2 * C_WIDTH:C_OFF + 3 * C_WIDTH]
    og = p_ref[bi, :, C_OFF + 3 * C_WIDTH:C_OFF + 4 * C_WIDTH]
    lb = lb_ref[...]
    f = lb + (1.0 - lb) * _sigmoid(fz)
    g = jnp.log(f)
    kd = 1.0 - f
    q = q_raw * _sigmoid(q_raw)
    yield
    b = _sel_left(tril, g)
    e_lev = [jnp.exp(x) for x in _level_decays(b)]
    b_last = _last_rows(b)
    qe = q * jnp.exp(b)
    kl = kd * jnp.exp(b_last - b)
    g_last = jnp.exp(b_last)
    dq = _head_sums(q * kd, bo)
    yield
    ti = lax.broadcasted_iota(jnp.int32, (t, PAIR), 0)
    lane = lax.broadcasted_iota(jnp.int32, (t, PAIR), 1)
    first = lane < C_HEAD
    sj = lane & (C_HEAD - 1)
    xs = jnp.where(ti > sj, ti ^ sj, 0)
    npair = C_WIDTH // PAIR
    cs = [(j, p) for j in range(SUB) for p in range(npair)]
    sl = lambda x, c: x[_rows(c[0]), c[1] * PAIR:(c[1] + 1) * PAIR]
    att = [jnp.zeros((t, PAIR), F32) for _ in cs]
    for l in range(C_LEVELS):
        el = e_lev[l]
        ql = q * el
        kl_l = kd * el
        lev = (xs >> l) == 1
        att = [att[i] + jnp.where(lev, _dot(sl(ql, c), _bd(sl(kl_l, c), first), 1, 1), 0.0)
               for i, c in enumerate(cs)]
        yield
    intra = [_dot(att[i], _bd(sl(v, c), first)) + sl(dq, c) * sl(v, c) for i, c in enumerate(cs)]
    upd = [jnp.where(bdm, _dot(sl(v, c), sl(kl, c), 0, 0), 0.0) for c in cs]
    yield
    sts = {}
    for p in range(npair):
        st = s_c[bi, p]
        for j in range(SUB):
            i = cs.index((j, p))
            sts[i] = st
            st = st * g_last[(j + 1) * t - 1:(j + 1) * t, p * PAIR:(p + 1) * PAIR] + upd[i]
        s_c[bi, p] = st
    outs = [intra[i] + _dot(sl(qe, c), sts[i], 1, 1) for i, c in enumerate(cs)]
    o = jnp.concatenate([jnp.concatenate([outs[cs.index((j, p))] for p in range(npair)], axis=1)
                         for j in range(SUB)], axis=0)
    yield
    ms = _head_sums(o * o, bo) * (1.0 / C_HEAD)
    o_ref[bi, :, A_WIDTH + B_WIDTH:D_MIX] = o * lax.rsqrt(ms + NORM_EPS) * cn_ref[...] * (og * _sigmoid(og))


def _group_b(bi, p_ref, o_ref, vf_ref, vfo_ref, mu_ref, wl_ref, w0_ref, a0_ref, kk_ref, ka_ref, rk_ref,
             lng_ref, lnb_ref, v0_ref, v1_ref, v2_ref, tril, bo, bdm, pbuf, s_b):
    t = CHUNK
    w = B_WIDTH
    pb = p_ref[bi, :, B_OFF:B_OFF + B_COLS]
    pbuf[bi, 8:8 + STEP, :] = pb
    prev = pbuf[bi, 7:7 + STEP, :]
    pbuf[bi, 0:8, :] = pbuf[bi, STEP:STEP + 8, :]
    sh = pb + mu_ref[...] * (prev - pb)
    r = sh[:, 0:w]
    k = sh[:, w:2 * w]
    v = sh[:, 2 * w:3 * w]
    lo = sh[:, 3 * w:3 * w + B_LORA]
    lane = lax.broadcasted_iota(jnp.int32, (STEP, B_LORA), 1)
    xl = jnp.where(lane < 32, jnp.tanh(lo), jnp.where(lane < 64, lo, _sigmoid(lo)))
    lora = _dot(xl, wl_ref[...])
    lw = -float(np.exp(-0.5)) * _sigmoid(w0_ref[...] + lora[:, 0:w])
    if vf_ref is None:
        vfo_ref[bi] = v
    else:
        mix = _sigmoid(v0_ref[...] + _dot(_dot(v, v1_ref[...]), v2_ref[...]))
        v = v + (vf_ref[bi] - v) * mix
    aa = _sigmoid(a0_ref[...] + lora[:, w:2 * w])
    gate = lora[:, 2 * w:3 * w]
    kk = k * kk_ref[...]
    k = k * (1.0 + (aa - 1.0) * ka_ref[...])
    sums = _head_sums(jnp.concatenate([kk * kk, r * k * rk_ref[...]], axis=0), bo)
    kk = kk * lax.rsqrt(jnp.maximum(sums[:STEP], KK_NORM_FLOOR ** 2))
    rk = sums[STEP:]
    ap = -kk
    bb = kk * aa
    c = _sel_left(tril, lw)
    c_last = _last_rows(c)
    a_h = ap * jnp.exp(c - lw)
    r_h = r * jnp.exp(c)
    e_inv = jnp.exp(-c)
    b_h = bb * e_inv
    k_h = k * e_inv
    e_cl = jnp.exp(c_last - c)
    b_c = bb * e_cl
    k_c = k * e_cl
    gam = jnp.exp(c_last)
    yield

    ti = lax.broadcasted_iota(jnp.int32, (t, PAIR), 0)
    lane = lax.broadcasted_iota(jnp.int32, (t, PAIR), 1)
    first = lane < B_HEAD
    sj = lane & (B_HEAD - 1)
    strict = ti > sj
    lower = ti >= sj
    blk = (ti >> 4) == (sj >> 4)
    eye = jnp.where(ti == sj, 1.0, 0.0)
    bd = lambda y: _bd(y, first)
    cat0 = lambda xs: jnp.concatenate(xs, axis=0)
    cat1 = lambda xs: jnp.concatenate(xs, axis=1)
    npair = B_WIDTH // PAIR
    cs = [(j, p) for j in range(SUB) for p in range(npair)]
    n = range(len(cs))
    sl = lambda x, i: x[_rows(cs[i][0]), cs[i][1] * PAIR:(cs[i][1] + 1) * PAIR]
    vbd = [bd(sl(v, i)) for i in n]
    ar = [cat0([sl(a_h, i), sl(r_h, i)]) for i in n]
    mm = [_dot(ar[i], cat0([bd(sl(b_h, i)), bd(sl(k_h, i))]), 1, 1) for i in n]
    yield
    m_ab = [jnp.where(strict, m[:t, :PAIR], 0.0) for m in mm]
    m_ak = [jnp.where(strict, m[:t, PAIR:], 0.0) for m in mm]
    n_rb = [jnp.where(lower, m[t:, :PAIR], 0.0) for m in mm]
    n_rk = [jnp.where(lower, m[t:, PAIR:], 0.0) for m in mm]
    dg = [jnp.where(blk, m, 0.0) for m in m_ab]
    off = [m_ab[i] - dg[i] for i in n]
    akv = [_dot(m_ak[i], vbd[i]) for i in n]
    nkv = [_dot(n_rk[i], vbd[i]) for i in n]
    d2 = [_dot(dg[i], bd(dg[i])) for i in n]
    yield
    s4 = [_dot(cat0([d2[i], dg[i]]), bd(d2[i])) for i in n]
    yield
    d4 = [x[:t] for x in s4]
    x1 = [eye + dg[i] + d2[i] + s4[i][t:] for i in n]
    s8 = [_dot(cat0([d4[i], x1[i]]), bd(d4[i])) for i in n]
    yield
    x2 = [x1[i] + s8[i][t:] for i in n]
    dinv = [x2[i] + _dot(x2[i], bd(s8[i][:t])) for i in n]
    yield
    gm = [_dot(dinv[i], bd(off[i])) for i in n]
    yield
    g2 = [_dot(gm[i], bd(gm[i])) for i in n]
    yield
    gs = [eye + gm[i] + g2[i] + _dot(gm[i], bd(g2[i])) for i in n]
    yield
    tinv = [_dot(gs[i], bd(dinv[i])) for i in n]
    yield
    pq = [_dot(tinv[i], cat1([bd(sl(a_h, i)), bd(akv[i])])) for i in n]
    yield
    npq = [_dot(n_rb[i], cat1([bd(pq[i][:, :PAIR]), bd(pq[i][:, PAIR:])])) for i in n]
    yield
    p2 = [sl(r_h, i) + npq[i][:, :PAIR] for i in n]
    q2 = [npq[i][:, PAIR:] + nkv[i] for i in n]
    wp = [jnp.where(bdm, _dot(pq[i][:, :PAIR], sl(b_c, i), 0, 0), 0.0) for i in n]
    zz = [jnp.where(bdm, _dot(cat0([pq[i][:, PAIR:], sl(v, i)]), cat0([sl(b_c, i), sl(k_c, i)]), 0, 0), 0.0)
          for i in n]
    yield
    sts = {}
    for p in range(npair):
        st = s_b[bi, p]
        for j in range(SUB):
            i = cs.index((j, p))
            sts[i] = st
            st = st * gam[(j + 1) * t - 1:(j + 1) * t, p * PAIR:(p + 1) * PAIR] + _dot(st, wp[i]) + zz[i]
        s_b[bi, p] = st
    ys = [_dot(p2[i], sts[i], 1, 1) + q2[i] for i in n]
    yield
    y = cat0([cat1([ys[cs.index((j, p))] for p in range(npair)]) for j in range(SUB)])
    inv_n = 1.0 / B_HEAD
    mean = _head_sums(y, bo) * inv_n
    yc = y - mean
    var = _head_sums(yc * yc, bo) * inv_n
    y = yc * lax.rsqrt(var + B_GN_EPS) * lng_ref[...] + lnb_ref[...]
    y = y + rk * v
    o_ref[bi, :, A_WIDTH:A_WIDTH + B_WIDTH] = y * gate


def _mixer_kernel(layer0, *refs):
    it = iter(refs)
    p_ref = next(it)
    vf_ref = None if layer0 else next(it)
    cw, cb, wri, bri, lam = (next(it) for _ in range(5))
    mu, wl, w0, a0, kkp, kap, rkp, lng, lnb = (next(it) for _ in range(9))
    v0 = v1 = v2 = None
    if not layer0:
        v0, v1, v2 = (next(it) for _ in range(3))
    lb, cn, tril_ref, bo_ref = (next(it) for _ in range(4))
    y_ref = next(it)
    vfo_ref = next(it) if layer0 else None
    xbuf, h_a, pbuf, s_b, s_c = (next(it) for _ in range(5))

    @pl.when(pl.program_id(1) == 0)
    def _():
        xbuf[:, 0:8, :] = jnp.zeros((NBATCH, 8, A_WIDTH), F32)
        pbuf[:, 0:8, :] = jnp.zeros((NBATCH, 8, B_COLS), F32)
        h_a[...] = jnp.zeros_like(h_a)
        s_b[...] = jnp.zeros_like(s_b)
        s_c[...] = jnp.zeros_like(s_c)

    tril = tril_ref[...]
    bo = bo_ref[...]
    half = lambda d: lax.broadcasted_iota(jnp.int32, (PAIR, PAIR), d) < PAIR // 2
    bdm = half(0) == half(1)
    live = []
    for bi in range(NBATCH):
        gb = _group_b(bi, p_ref, y_ref, vf_ref, vfo_ref, mu, wl, w0, a0, kkp, kap, rkp, lng, lnb, v0, v1, v2,
                      tril, bo, bdm, pbuf, s_b)
        gc = _group_c(bi, p_ref, y_ref, lb, cn, tril, bo, bdm, s_c)
        ga = _group_a(bi, p_ref, y_ref, cw, cb, wri, bri, lam, xbuf, h_a)
        next(gb)
        live += [gb, gc, ga]
        rounds = STAGGER if bi + 1 < NBATCH else 1 << 30
        while live and rounds:
            rounds -= 1
            for gen in list(live):
                try:
                    next(gen)
                except StopIteration:
                    live.remove(gen)


def _step_constants():
    eye = np.eye(SUB, dtype=np.float32)
    tril = np.kron(eye, np.tril(np.ones((CHUNK, CHUNK), np.float32)))
    hid = np.arange(PAIR) // B_HEAD
    bo = (hid[:, None] == hid[None, :]).astype(np.float32)
    return tuple(jnp.asarray(m, BF16) for m in (tril, bo))


def _mixer(proj, v_first, params, consts, layer0):
    bsz, lp, _ = proj.shape
    t = STEP
    const = lambda b, c: (0, 0)
    row = lambda b, c: (b, c, 0)
    ins = [proj]
    specs = [pl.BlockSpec((NBATCH, t, N_IN), row)]
    if not layer0:
        ins.append(v_first)
        specs.append(pl.BlockSpec((NBATCH, t, B_WIDTH), row))
    for arr, layer in params:
        ins.append(arr)
        specs.append(_layer_spec(arr, layer))
    for arr in consts:
        ins.append(arr)
        specs.append(pl.BlockSpec(arr.shape, const))
    out_shape = [jax.ShapeDtypeStruct((bsz, lp, D_MIX), F32)]
    out_specs = [pl.BlockSpec((NBATCH, t, D_MIX), row)]
    if layer0:
        out_shape.append(jax.ShapeDtypeStruct((bsz, lp, B_WIDTH), F32))
        out_specs.append(pl.BlockSpec((NBATCH, t, B_WIDTH), row))
    res = pl.pallas_call(
        functools.partial(_mixer_kernel, layer0),
        grid=(bsz // NBATCH, lp // t),
        in_specs=specs,
        out_specs=out_specs,
        out_shape=out_shape,
        scratch_shapes=[pltpu.VMEM((NBATCH, t + 8, A_WIDTH), F32),
                        pltpu.VMEM((NBATCH, 8, A_WIDTH), F32),
                        pltpu.VMEM((NBATCH, t + 8, B_COLS), F32),
                        pltpu.VMEM((NBATCH, B_WIDTH // PAIR, PAIR, PAIR), F32),
                        pltpu.VMEM((NBATCH, C_WIDTH // PAIR, PAIR, PAIR), F32)],
        compiler_params=pltpu.CompilerParams(dimension_semantics=("arbitrary", "arbitrary"),
                                             vmem_limit_bytes=V7X_VMEM_LIMIT),
        name="mixer0" if layer0 else "mixer",
    )(*ins)
    return res if layer0 else (res[0], v_first)


def _block_diag(w):
    nl, hn, n, _ = w.shape
    out = jnp.zeros((nl, hn * n, hn * n), w.dtype)
    for h in range(hn):
        out = out.at[:, h * n:(h + 1) * n, h * n:(h + 1) * n].set(w[:, h])
    return out


def kernel(x, meta, mix_norm, w_in, a_conv_w, a_conv_b, a_w_r, a_b_r, a_w_i, a_b_i, a_lambda, b_mu, b_w0, b_w2, b_a0, b_a2, b_g2, b_k_k, b_k_a, b_r_k, b_ln_g, b_ln_b, b_v0, b_v1, b_v2, c_lb, c_norm, w_out, ffn_norm, w_up, w_down, final_norm):
    bsz, seq, d = x.shape
    depth = w_in.shape[0]
    l_real = N_META + seq
    lp = -(-l_real // STEP) * STEP
    while (bsz * lp) % ROW_TILE:
        lp += STEP
    lb_all = jnp.cumsum(jax.nn.softmax(c_lb.astype(F32), axis=0), axis=0)
    lb_all = lb_all - lb_all[0]
    consts = _step_constants()
    row = lambda a: a.reshape(a.shape[0], 1, -1)
    pair_bd = lambda w: _block_diag(w.reshape(depth * (A_HEADS // 2), 2, *w.shape[2:]))
    wri = jnp.concatenate([pair_bd(a_w_r), pair_bd(a_w_i)], axis=2).astype(BF16)
    wri = wri.reshape(depth, A_HEADS // 2, PAIR, 2 * PAIR)
    bri = row(jnp.concatenate([a_b_r, a_b_i], axis=1))
    cols = lambda a, i: jnp.pad(a, ((0, 0), (0, 0), (i * B_WIDTH, (2 - i) * B_WIDTH)))
    wl = jnp.concatenate([cols(b_w2, 0), cols(b_a2, 1), cols(b_g2, 2)], axis=1).astype(BF16)
    shared = [a_conv_w, row(a_conv_b), wri, bri, row(a_lambda), row(b_mu), wl, row(b_w0), row(b_a0),
              row(b_k_k), row(b_k_a), row(b_r_k), row(b_ln_g), row(b_ln_b)]
    vmix = [row(b_v0), b_v1.astype(BF16), b_v2.astype(BF16)]
    tail = [row(lb_all), row(jnp.tile(c_norm, (1, C_HEADS)))]
    g_mix, g_ffn = row(mix_norm), row(ffn_norm)
    w_up_b, w_down_b = w_up.astype(BF16), w_down.astype(BF16)

    v_first = None
    for l in range(depth):
        if l == 0:
            proj, h = _proj_first(x, meta.astype(x.dtype), g_mix, w_in, lp)
            h = h.reshape(bsz * lp, d)
        else:
            proj = _proj(h, g_mix, w_in, l).reshape(bsz, lp, N_IN)
        params = [(a, l) for a in shared] + ([(a, l - 1) for a in vmix] if l else []) + [(a, l) for a in tail]
        y, v_first = _mixer(proj, v_first, params, consts, l == 0)
        if l < depth - 1:
            h = _post(h, y.reshape(bsz * lp, D_MIX), w_out, g_ffn, w_up_b, w_down_b, l)
    return _post_last(h.reshape(bsz, lp, d), y, w_out, g_ffn, w_up_b, w_down_b,
                      final_norm.reshape(1, -1), depth - 1, seq)
```

```python
import functools

import numpy as np
import jax
import jax.numpy as jnp
from jax import lax
from jax.experimental import pallas as pl
from jax.experimental.pallas import tpu as pltpu

F32 = jnp.float32
BF16 = jnp.bfloat16

N_META = 16
NORM_EPS = 1e-6
CHUNK = 64
SUB = 3
NBATCH = 4
STEP = SUB * CHUNK

A_WIDTH = 384
A_HEADS = 6
A_C = 8.0
B_WIDTH = 384
B_HEAD = 64
B_HEADS = 6
B_LORA = 128
B_COLS = 3 * B_WIDTH + B_LORA
B_GN_EPS = 64e-5
C_WIDTH = 256
C_HEADS = 4
C_HEAD = 64
C_LEVELS = 6
PAIR = 128
STAGGER = 4
A_OFF = 0
B_OFF = 2 * A_WIDTH
C_OFF = B_OFF + B_COLS
N_IN = C_OFF + 4 * C_WIDTH
D_MIX = A_WIDTH + B_WIDTH + C_WIDTH

V7X_VMEM_LIMIT = 56 * 1024 * 1024
F_CHUNK = 1024
TINY = 1e-30
KK_NORM_FLOOR = 1e-12
ROW_TILE = 704


def _dot(a, b, ca=1, cb=0):
    dims = (((ca,), (cb,)), ((), ()))
    return lax.dot_general(a.astype(BF16), b.astype(BF16), dims, preferred_element_type=F32)


def _split2(x):
    hi = x.astype(BF16)
    return hi, (x - hi.astype(F32)).astype(BF16)


def _sel_left(m, x):
    n = x.shape[1]
    r = jnp.dot(m, jnp.concatenate(_split2(x), axis=1), preferred_element_type=F32)
    return r[:, :n] + r[:, n:]


def _bd(y, first):
    return jnp.concatenate([jnp.where(first, y, 0.0), jnp.where(first, 0.0, y)], axis=0)


def _head_sums(x, bo):
    npair = x.shape[1] // PAIR
    rows = x.shape[0]
    r = _dot(jnp.concatenate([x[:, p * PAIR:(p + 1) * PAIR] for p in range(npair)], axis=0), bo)
    return jnp.concatenate([r[p * rows:(p + 1) * rows] for p in range(npair)], axis=1)


def _sigmoid(x):
    return jax.nn.sigmoid(x)


def _softplus(x):
    return jnp.maximum(x, 0.0) + jnp.log1p(jnp.exp(-jnp.abs(x)))


def _rmsnorm(x, g):
    return x * lax.rsqrt(jnp.mean(x * x, axis=-1, keepdims=True) + NORM_EPS) * g


def _proj_kernel(h_ref, g_ref, w_ref, o_ref, wb):
    @pl.when(pl.program_id(0) == 0)
    def _():
        wb[...] = w_ref[...].astype(BF16)

    xn = _rmsnorm(h_ref[...], g_ref[...])
    o_ref[...] = jnp.dot(xn.astype(BF16), wb[...], preferred_element_type=F32)


def _proj_first_kernel(l_real, x_ref, xp_ref, meta_ref, g_ref, w_ref, o_ref, h_ref, wb):
    j = pl.program_id(1)

    @pl.when((pl.program_id(0) == 0) & (j == 0))
    def _():
        wb[...] = w_ref[...].astype(BF16)

    top = jnp.where(j == 0, meta_ref[...], xp_ref[...])
    h = jnp.concatenate([top, x_ref[0:ROW_TILE - N_META, :]], axis=0)
    t = j * ROW_TILE + lax.broadcasted_iota(jnp.int32, h.shape, 0)
    h = jnp.where(t < l_real, h, 0.0)
    h_ref[...] = h
    xn = _rmsnorm(h, g_ref[...])
    o_ref[...] = jnp.dot(xn.astype(BF16), wb[...], preferred_element_type=F32)


def _proj_first(x, meta, g, w, lp):
    bsz, seq, d = x.shape
    n = w.shape[2]
    t = ROW_TILE
    per = t // N_META
    last_main = -(-seq // t) - 1
    last_prev = seq // N_META - 1
    assert lp // t == last_main + 1, "every tile of the padded length must overlap x"
    return pl.pallas_call(
        functools.partial(_proj_first_kernel, N_META + seq),
        grid=(bsz, lp // t),
        in_specs=[pl.BlockSpec((None, t, d), lambda b, j: (b, jnp.minimum(j, last_main), 0)),
                  pl.BlockSpec((None, N_META, d),
                               lambda b, j: (b, jnp.clip(j * per - 1, 0, last_prev), 0)),
                  pl.BlockSpec(meta.shape, lambda b, j: (0, 0)),
                  _layer_spec(g, 0), _layer_spec(w, 0, single=True)],
        out_specs=[pl.BlockSpec((None, t, n), lambda b, j: (b, j, 0)),
                   pl.BlockSpec((None, t, d), lambda b, j: (b, j, 0))],
        out_shape=[jax.ShapeDtypeStruct((bsz, lp, n), F32), jax.ShapeDtypeStruct((bsz, lp, d), F32)],
        scratch_shapes=[pltpu.VMEM((d, n), BF16)],
        compiler_params=pltpu.CompilerParams(dimension_semantics=("arbitrary", "arbitrary"),
                                             vmem_limit_bytes=V7X_VMEM_LIMIT),
        name="proj_first",
    )(x, x, meta, g, w)


def _layer_spec(arr, layer, single=False):
    zeros = (0,) * (arr.ndim - 1)
    kw = dict(pipeline_mode=pl.Buffered(1)) if single else {}
    return pl.BlockSpec((None,) + arr.shape[1:], lambda *_: (layer,) + zeros, **kw)


def _proj(h2d, g, w, layer):
    rows, d = h2d.shape
    n = w.shape[2]
    return pl.pallas_call(
        _proj_kernel,
        grid=(rows // ROW_TILE,),
        in_specs=[pl.BlockSpec((ROW_TILE, d), lambda i: (i, 0)),
                  _layer_spec(g, layer), _layer_spec(w, layer, single=True)],
        out_specs=pl.BlockSpec((ROW_TILE, n), lambda i: (i, 0)),
        out_shape=jax.ShapeDtypeStruct((rows, n), F32),
        scratch_shapes=[pltpu.VMEM((d, n), BF16)],
        compiler_params=pltpu.CompilerParams(dimension_semantics=("arbitrary",),
                                             vmem_limit_bytes=V7X_VMEM_LIMIT),
        name="proj",
    )(h2d, g, w)


def _residual_mlp(hs, ys, wob, g_ref, wu_ref, wd_ref, f_chunk):
    h1 = hs + jnp.dot(ys.astype(BF16), wob[...], preferred_element_type=F32)
    hb = _rmsnorm(h1, g_ref[...]).astype(BF16)
    acc = h1
    d_ff = wu_ref.shape[1]
    for c in range(d_ff // f_chunk):
        u = jnp.dot(hb, wu_ref[:, c * f_chunk:(c + 1) * f_chunk], preferred_element_type=F32)
        a = jnp.square(jnp.maximum(u, 0.0)).astype(BF16)
        acc = acc + jnp.dot(a, wd_ref[c * f_chunk:(c + 1) * f_chunk, :],
                            preferred_element_type=F32)
    return acc


def _post_kernel(f_chunk, h_ref, y_ref, wo_ref, g_ref, wu_ref, wd_ref, o_ref, wob):
    @pl.when(pl.program_id(0) == 0)
    def _():
        wob[...] = wo_ref[...].astype(BF16)

    o_ref[...] = _residual_mlp(h_ref[...], y_ref[...], wob, g_ref, wu_ref, wd_ref, f_chunk)


def _post(h2d, y2d, wo, g, wu, wd, layer):
    rows, d = h2d.shape
    tile = ROW_TILE
    return pl.pallas_call(
        functools.partial(_post_kernel, F_CHUNK),
        grid=(rows // tile,),
        in_specs=[pl.BlockSpec((tile, d), lambda i: (i, 0)),
                  pl.BlockSpec((tile, D_MIX), lambda i: (i, 0)),
                  _layer_spec(wo, layer, single=True), _layer_spec(g, layer),
                  _layer_spec(wu, layer, single=True), _layer_spec(wd, layer, single=True)],
        out_specs=pl.BlockSpec((tile, d), lambda i: (i, 0)),
        out_shape=jax.ShapeDtypeStruct((rows, d), F32),
        scratch_shapes=[pltpu.VMEM((D_MIX, d), BF16)],
        compiler_params=pltpu.CompilerParams(dimension_semantics=("arbitrary",),
                                             vmem_limit_bytes=V7X_VMEM_LIMIT),
        name="post",
    )(h2d, y2d, wo, g, wu, wd)


def _post_last_kernel(f_chunk, h_ref, hx_ref, y_ref, yx_ref, wo_ref, g_ref, wu_ref, wd_ref, gf_ref, o_ref, wob):
    @pl.when((pl.program_id(0) == 0) & (pl.program_id(1) == 0))
    def _():
        wob[...] = wo_ref[...].astype(BF16)

    shift = lambda main, extra: jnp.concatenate([main[N_META:, :], extra[...]], axis=0)
    acc = _residual_mlp(shift(h_ref, hx_ref), shift(y_ref, yx_ref), wob, g_ref, wu_ref, wd_ref, f_chunk)
    o_ref[...] = _rmsnorm(acc, gf_ref[...])


def _post_last(h3d, y3d, wo, g, wu, wd, gf, layer, seq):
    bsz, lp, d = h3d.shape
    t = ROW_TILE
    per = t // N_META
    last_extra = lp // N_META - 1
    assert -(-seq // t) == lp // t, "every tile of the padded length must overlap the output"
    main = lambda b, j: (b, j, 0)
    extra = lambda b, j: (b, jnp.minimum((j + 1) * per, last_extra), 0)
    return pl.pallas_call(
        functools.partial(_post_last_kernel, F_CHUNK),
        grid=(bsz, lp // t),
        in_specs=[pl.BlockSpec((None, t, d), main), pl.BlockSpec((None, N_META, d), extra),
                  pl.BlockSpec((None, t, D_MIX), main), pl.BlockSpec((None, N_META, D_MIX), extra),
                  _layer_spec(wo, layer, single=True), _layer_spec(g, layer),
                  _layer_spec(wu, layer, single=True), _layer_spec(wd, layer, single=True),
                  pl.BlockSpec(gf.shape, lambda b, j: (0, 0))],
        out_specs=pl.BlockSpec((None, t, d), main),
        out_shape=jax.ShapeDtypeStruct((bsz, seq, d), F32),
        scratch_shapes=[pltpu.VMEM((D_MIX, d), BF16)],
        compiler_params=pltpu.CompilerParams(dimension_semantics=("arbitrary", "arbitrary"),
                                             vmem_limit_bytes=V7X_VMEM_LIMIT),
        name="post_last",
    )(h3d, h3d, y3d, y3d, wo, g, wu, wd, gf)


def _rows(j):
    return slice(j * CHUNK, (j + 1) * CHUNK)


def _last_rows(x):
    return jnp.concatenate([jnp.broadcast_to(x[(j + 1) * CHUNK - 1:(j + 1) * CHUNK, :], (CHUNK, x.shape[1]))
                            for j in range(SUB)], axis=0)


def _level_decays(b):
    rows, n = b.shape
    out = []
    sub = lax.broadcasted_iota(jnp.int32, (rows, n), 0) & 7
    b3 = b.reshape(rows // 8, 8, n)
    pick = lambda k: jnp.broadcast_to(b3[:, k:k + 1, :], b3.shape).reshape(rows, n)
    for l in range(C_LEVELS):
        half = 1 << l
        if l == 0:
            m = jnp.where((sub & 1) == 1, pltpu.roll(b, 1, 0), b)
        elif l == 1:
            m = jnp.where(sub < 4, pick(1), pick(5))
        elif l == 2:
            m = pick(3)
        else:
            blk = 2 * half
            m = jnp.concatenate([jnp.broadcast_to(b[i * blk + half - 1:i * blk + half, :], (blk, n))
                                 for i in range(rows // blk)], axis=0)
        out.append(-jnp.abs(b - m))
    return out


def _group_a(bi, p_ref, o_ref, cw_ref, cb_ref, wri_ref, bri_ref, lam_ref, xbuf, h_a):
    t = STEP
    xa = p_ref[bi, :, A_OFF:A_OFF + A_WIDTH]
    ga = p_ref[bi, :, A_OFF + A_WIDTH:A_OFF + 2 * A_WIDTH]
    xbuf[bi, 8:8 + t, :] = xa
    u = cb_ref[...]
    for j in range(4):
        u = u + cw_ref[j:j + 1, :] * xbuf[bi, 5 + j:5 + j + t, :]
    xbuf[bi, 0:8, :] = xbuf[bi, t:t + 8, :]
    yield
    ri = [_dot(u[:, p * PAIR:(p + 1) * PAIR], wri_ref[p]) for p in range(A_WIDTH // PAIR)]
    r = _sigmoid(jnp.concatenate([x[:, :PAIR] for x in ri], axis=1) + bri_ref[:, :A_WIDTH])
    i = _sigmoid(jnp.concatenate([x[:, PAIR:] for x in ri], axis=1) + bri_ref[:, A_WIDTH:])
    yield
    log_a = (-A_C * r) * _softplus(-lam_ref[...])
    a = jnp.exp(log_a)
    om = 1.0 - a * a
    d = (om * lax.rsqrt(jnp.maximum(om, TINY))) * (i * u)
    yield
    rows = lax.broadcasted_iota(jnp.int32, (t, A_WIDTH), 0) & 7
    for s in (1, 2, 4):
        a_sh = jnp.where(rows < s, 1.0, pltpu.roll(a, s, 0))
        d_sh = jnp.where(rows < s, 0.0, pltpu.roll(d, s, 0))
        d = a * d_sh + d
        a = a * a_sh
        yield
    carry = h_a[bi, 0:1, :]
    hs = []
    for g in range(t // 8):
        hg = d[8 * g:8 * g + 8, :] + a[8 * g:8 * g + 8, :] * carry
        carry = hg[7:8, :]
        hs.append(hg)
        if g % 8 == 7:
            yield
    hseq = jnp.concatenate(hs, axis=0)
    h_a[bi, 0:1, :] = carry
    o_ref[bi, :, 0:A_WIDTH] = hseq * jax.nn.gelu(ga, approximate=True)


def _group_c(bi, p_ref, o_ref, lb_ref, cn_ref, tril, bo, bdm, s_c):
    t = CHUNK
    q_raw = p_ref[bi, :, C_OFF:C_OFF + C_WIDTH]
    fz = p_ref[bi, :, C_OFF + C_WIDTH:C_OFF + 2 * C_WIDTH]
    v = p_ref[bi, :, C_OFF + 2 * C_WIDTH:C_OFF + 3 * C_WIDTH]
    og = p_ref[bi, :, C_OFF + 3 * C_WIDTH:C_OFF + 4 * C_WIDTH]
    lb = lb_ref[...]
    f = lb + (1.0 - lb) * _sigmoid(fz)
    g = jnp.log(f)
    kd = 1.0 - f
    q = q_raw * _sigmoid(q_raw)
    yield
    b = _sel_left(tril, g)
    e_lev = [jnp.exp(x) for x in _level_decays(b)]
    b_last = _last_rows(b)
    qe = q * jnp.exp(b)
    kl = kd * jnp.exp(b_last - b)
    g_last = jnp.exp(b_last)
    dq = _head_sums(q * kd, bo)
    yield
    ti = lax.broadcasted_iota(jnp.int32, (t, PAIR), 0)
    lane = lax.broadcasted_iota(jnp.int32, (t, PAIR), 1)
    first = lane < C_HEAD
    sj = lane & (C_HEAD - 1)
    xs = jnp.where(ti > sj, ti ^ sj, 0)
    npair = C_WIDTH // PAIR
    cs = [(j, p) for j in range(SUB) for p in range(npair)]
    sl = lambda x, c: x[_rows(c[0]), c[1] * PAIR:(c[1] + 1) * PAIR]
    att = [jnp.zeros((t, PAIR), F32) for _ in cs]
    for l in range(C_LEVELS):
        el = e_lev[l]
        ql = q * el
        kl_l = kd * el
        lev = (xs >> l) == 1
        att = [att[i] + jnp.where(lev, _dot(sl(ql, c), _bd(sl(kl_l, c), first), 1, 1), 0.0)
               for i, c in enumerate(cs)]
        yield
    intra = [_dot(att[i], _bd(sl(v, c), first)) + sl(dq, c) * sl(v, c) for i, c in enumerate(cs)]
    upd = [jnp.where(bdm, _dot(sl(v, c), sl(kl, c), 0, 0), 0.0) for c in cs]
    yield
    sts = {}
    for p in range(npair):
        st = s_c[bi, p]
        for j in range(SUB):
            i = cs.index((j, p))
            sts[i] = st
            st = st * g_last[(j + 1) * t - 1:(j + 1) * t, p * PAIR:(p + 1) * PAIR] + upd[i]
        s_c[bi, p] = st
    outs = [intra[i] + _dot(sl(qe, c), sts[i], 1, 1) for i, c in enumerate(cs)]
    o = jnp.concatenate([jnp.concatenate([outs[cs.index((j, p))] for p in range(npair)], axis=1)
                         for j in range(SUB)], axis=0)
    yield
    ms = _head_sums(o * o, bo) * (1.0 / C_HEAD)
    o_ref[bi, :, A_WIDTH + B_WIDTH:D_MIX] = o * lax.rsqrt(ms + NORM_EPS) * cn_ref[...] * (og * _sigmoid(og))


def _group_b(bi, p_ref, o_ref, vf_ref, vfo_ref, mu_ref, wl_ref, w0_ref, a0_ref, kk_ref, ka_ref, rk_ref,
             lng_ref, lnb_ref, v0_ref, v1_ref, v2_ref, tril, bo, bdm, pbuf, s_b):
    t = CHUNK
    w = B_WIDTH
    pb = p_ref[bi, :, B_OFF:B_OFF + B_COLS]
    pbuf[bi, 8:8 + STEP, :] = pb
    prev = pbuf[bi, 7:7 + STEP, :]
    pbuf[bi, 0:8, :] = pbuf[bi, STEP:STEP + 8, :]
    sh = pb + mu_ref[...] * (prev - pb)
    r = sh[:, 0:w]
    k = sh[:, w:2 * w]
    v = sh[:, 2 * w:3 * w]
    lo = sh[:, 3 * w:3 * w + B_LORA]
    lane = lax.broadcasted_iota(jnp.int32, (STEP, B_LORA), 1)
    xl = jnp.where(lane < 32, jnp.tanh(lo), jnp.where(lane < 64, lo, _sigmoid(lo)))
    lora = _dot(xl, wl_ref[...])
    lw = -float(np.exp(-0.5)) * _sigmoid(w0_ref[...] + lora[:, 0:w])
    if vf_ref is None:
        vfo_ref[bi] = v
    else:
        mix = _sigmoid(v0_ref[...] + _dot(_dot(v, v1_ref[...]), v2_ref[...]))
        v = v + (vf_ref[bi] - v) * mix
    aa = _sigmoid(a0_ref[...] + lora[:, w:2 * w])
    gate = lora[:, 2 * w:3 * w]
    kk = k * kk_ref[...]
    k = k * (1.0 + (aa - 1.0) * ka_ref[...])
    sums = _head_sums(jnp.concatenate([kk * kk, r * k * rk_ref[...]], axis=0), bo)
    kk = kk * lax.rsqrt(jnp.maximum(sums[:STEP], KK_NORM_FLOOR ** 2))
    rk = sums[STEP:]
    ap = -kk
    bb = kk * aa
    c = _sel_left(tril, lw)
    c_last = _last_rows(c)
    a_h = ap * jnp.exp(c - lw)
    r_h = r * jnp.exp(c)
    e_inv = jnp.exp(-c)
    b_h = bb * e_inv
    k_h = k * e_inv
    e_cl = jnp.exp(c_last - c)
    b_c = bb * e_cl
    k_c = k * e_cl
    gam = jnp.exp(c_last)
    yield

    ti = lax.broadcasted_iota(jnp.int32, (t, PAIR), 0)
    lane = lax.broadcasted_iota(jnp.int32, (t, PAIR), 1)
    first = lane < B_HEAD
    sj = lane & (B_HEAD - 1)
    strict = ti > sj
    lower = ti >= sj
    blk = (ti >> 4) == (sj >> 4)
    eye = jnp.where(ti == sj, 1.0, 0.0)
    bd = lambda y: _bd(y, first)
    cat0 = lambda xs: jnp.concatenate(xs, axis=0)
    cat1 = lambda xs: jnp.concatenate(xs, axis=1)
    npair = B_WIDTH // PAIR
    cs = [(j, p) for j in range(SUB) for p in range(npair)]
    n = range(len(cs))
    sl = lambda x, i: x[_rows(cs[i][0]), cs[i][1] * PAIR:(cs[i][1] + 1) * PAIR]
    vbd = [bd(sl(v, i)) for i in n]
    ar = [cat0([sl(a_h, i), sl(r_h, i)]) for i in n]
    mm = [_dot(ar[i], cat0([bd(sl(b_h, i)), bd(sl(k_h, i))]), 1, 1) for i in n]
    yield
    m_ab = [jnp.where(strict, m[:t, :PAIR], 0.0) for m in mm]
    m_ak = [jnp.where(strict, m[:t, PAIR:], 0.0) for m in mm]
    n_rb = [jnp.where(lower, m[t:, :PAIR], 0.0) for m in mm]
    n_rk = [jnp.where(lower, m[t:, PAIR:], 0.0) for m in mm]
    dg = [jnp.where(blk, m, 0.0) for m in m_ab]
    off = [m_ab[i] - dg[i] for i in n]
    akv = [_dot(m_ak[i], vbd[i]) for i in n]
    nkv = [_dot(n_rk[i], vbd[i]) for i in n]
    d2 = [_dot(dg[i], bd(dg[i])) for i in n]
    yield
    s4 = [_dot(cat0([d2[i], dg[i]]), bd(d2[i])) for i in n]
    yield
    d4 = [x[:t] for x in s4]
    x1 = [eye + dg[i] + d2[i] + s4[i][t:] for i in n]
    s8 = [_dot(cat0([d4[i], x1[i]]), bd(d4[i])) for i in n]
    yield
    x2 = [x1[i] + s8[i][t:] for i in n]
    dinv = [x2[i] + _dot(x2[i], bd(s8[i][:t])) for i in n]
    yield
    gm = [_dot(dinv[i], bd(off[i])) for i in n]
    yield
    g2 = [_dot(gm[i], bd(gm[i])) for i in n]
    yield
    gs = [eye + gm[i] + g2[i] + _dot(gm[i], bd(g2[i])) for i in n]
    yield
    tinv = [_dot(gs[i], bd(dinv[i])) for i in n]
    yield
    pq = [_dot(tinv[i], cat1([bd(sl(a_h, i)), bd(akv[i])])) for i in n]
    yield
    npq = [_dot(n_rb[i], cat1([bd(pq[i][:, :PAIR]), bd(pq[i][:, PAIR:])])) for i in n]
    yield
    p2 = [sl(r_h, i) + npq[i][:, :PAIR] for i in n]
    q2 = [npq[i][:, PAIR:] + nkv[i] for i in n]
    wp = [jnp.where(bdm, _dot(pq[i][:, :PAIR], sl(b_c, i), 0, 0), 0.0) for i in n]
    zz = [jnp.where(bdm, _dot(cat0([pq[i][:, PAIR:], sl(v, i)]), cat0([sl(b_c, i), sl(k_c, i)]), 0, 0), 0.0)
          for i in n]
    yield
    sts = {}
    for p in range(npair):
        st = s_b[bi, p]
        for j in range(SUB):
            i = cs.index((j, p))
            sts[i] = st
            st = st * gam[(j + 1) * t - 1:(j + 1) * t, p * PAIR:(p + 1) * PAIR] + _dot(st, wp[i]) + zz[i]
        s_b[bi, p] = st
    ys = [_dot(p2[i], sts[i], 1, 1) + q2[i] for i in n]
    yield
    y = cat0([cat1([ys[cs.index((j, p))] for p in range(npair)]) for j in range(SUB)])
    inv_n = 1.0 / B_HEAD
    mean = _head_sums(y, bo) * inv_n
    yc = y - mean
    var = _head_sums(yc * yc, bo) * inv_n
    y = yc * lax.rsqrt(var + B_GN_EPS) * lng_ref[...] + lnb_ref[...]
    y = y + rk * v
    o_ref[bi, :, A_WIDTH:A_WIDTH + B_WIDTH] = y * gate


def _mixer_kernel(layer0, *refs):
    it = iter(refs)
    p_ref = next(it)
    vf_ref = None if layer0 else next(it)
    cw, cb, wri, bri, lam = (next(it) for _ in range(5))
    mu, wl, w0, a0, kkp, kap, rkp, lng, lnb = (next(it) for _ in range(9))
    v0 = v1 = v2 = None
    if not layer0:
        v0, v1, v2 = (next(it) for _ in range(3))
    lb, cn, tril_ref, bo_ref = (next(it) for _ in range(4))
    y_ref = next(it)
    vfo_ref = next(it) if layer0 else None
    xbuf, h_a, pbuf, s_b, s_c = (next(it) for _ in range(5))

    @pl.when(pl.program_id(1) == 0)
    def _():
        xbuf[:, 0:8, :] = jnp.zeros((NBATCH, 8, A_WIDTH), F32)
        pbuf[:, 0:8, :] = jnp.zeros((NBATCH, 8, B_COLS), F32)
        h_a[...] = jnp.zeros_like(h_a)
        s_b[...] = jnp.zeros_like(s_b)
        s_c[...] = jnp.zeros_like(s_c)

    tril = tril_ref[...]
    bo = bo_ref[...]
    half = lambda d: lax.broadcasted_iota(jnp.int32, (PAIR, PAIR), d) < PAIR // 2
    bdm = half(0) == half(1)
    live = []
    for bi in range(NBATCH):
        gb = _group_b(bi, p_ref, y_ref, vf_ref, vfo_ref, mu, wl, w0, a0, kkp, kap, rkp, lng, lnb, v0, v1, v2,
                      tril, bo, bdm, pbuf, s_b)
        gc = _group_c(bi, p_ref, y_ref, lb, cn, tril, bo, bdm, s_c)
        ga = _group_a(bi, p_ref, y_ref, cw, cb, wri, bri, lam, xbuf, h_a)
        next(gb)
        live += [gb, gc, ga]
        rounds = STAGGER if bi + 1 < NBATCH else 1 << 30
        while live and rounds:
            rounds -= 1
            for gen in list(live):
                try:
                    next(gen)
                except StopIteration:
                    live.remove(gen)


def _step_constants():
    eye = np.eye(SUB, dtype=np.float32)
    tril = np.kron(eye, np.tril(np.ones((CHUNK, CHUNK), np.float32)))
    hid = np.arange(PAIR) // B_HEAD
    bo = (hid[:, None] == hid[None, :]).astype(np.float32)
    return tuple(jnp.asarray(m, BF16) for m in (tril, bo))


def _mixer(proj, v_first, params, consts, layer0):
    bsz, lp, _ = proj.shape
    t = STEP
    const = lambda b, c: (0, 0)
    row = lambda b, c: (b, c, 0)
    ins = [proj]
    specs = [pl.BlockSpec((NBATCH, t, N_IN), row)]
    if not layer0:
        ins.append(v_first)
        specs.append(pl.BlockSpec((NBATCH, t, B_WIDTH), row))
    for arr, layer in params:
        ins.append(arr)
        specs.append(_layer_spec(arr, layer))
    for arr in consts:
        ins.append(arr)
        specs.append(pl.BlockSpec(arr.shape, const))
    out_shape = [jax.ShapeDtypeStruct((bsz, lp, D_MIX), F32)]
    out_specs = [pl.BlockSpec((NBATCH, t, D_MIX), row)]
    if layer0:
        out_shape.append(jax.ShapeDtypeStruct((bsz, lp, B_WIDTH), F32))
        out_specs.append(pl.BlockSpec((NBATCH, t, B_WIDTH), row))
    res = pl.pallas_call(
        functools.partial(_mixer_kernel, layer0),
        grid=(bsz // NBATCH, lp // t),
        in_specs=specs,
        out_specs=out_specs,
        out_shape=out_shape,
        scratch_shapes=[pltpu.VMEM((NBATCH, t + 8, A_WIDTH), F32),
                        pltpu.VMEM((NBATCH, 8, A_WIDTH), F32),
                        pltpu.VMEM((NBATCH, t + 8, B_COLS), F32),
                        pltpu.VMEM((NBATCH, B_WIDTH // PAIR, PAIR, PAIR), F32),
                        pltpu.VMEM((NBATCH, C_WIDTH // PAIR, PAIR, PAIR), F32)],
        compiler_params=pltpu.CompilerParams(dimension_semantics=("arbitrary", "arbitrary"),
                                             vmem_limit_bytes=V7X_VMEM_LIMIT),
        name="mixer0" if layer0 else "mixer",
    )(*ins)
    return res if layer0 else (res[0], v_first)


def _block_diag(w):
    nl, hn, n, _ = w.shape
    return jnp.concatenate([jnp.pad(w[:, h], ((0, 0), (0, 0), (h * n, (hn - 1 - h) * n))) for h in range(hn)],
                           axis=1)


def kernel(x, meta, mix_norm, w_in, a_conv_w, a_conv_b, a_w_r, a_b_r, a_w_i, a_b_i, a_lambda, b_mu, b_w0, b_w2, b_a0, b_a2, b_g2, b_k_k, b_k_a, b_r_k, b_ln_g, b_ln_b, b_v0, b_v1, b_v2, c_lb, c_norm, w_out, ffn_norm, w_up, w_down, final_norm):
    bsz, seq, d = x.shape
    depth = w_in.shape[0]
    l_real = N_META + seq
    lp = -(-l_real // STEP) * STEP
    while (bsz * lp) % ROW_TILE:
        lp += STEP
    lb_all = jnp.cumsum(jax.nn.softmax(c_lb.astype(F32), axis=0), axis=0)
    lb_all = lb_all - lb_all[0]
    consts = _step_constants()
    row = lambda a: a.reshape(a.shape[0], 1, -1)
    pair_bd = lambda w: _block_diag(w.reshape(depth * (A_HEADS // 2), 2, *w.shape[2:]))
    wri = jnp.concatenate([pair_bd(a_w_r), pair_bd(a_w_i)], axis=2).astype(BF16)
    wri = wri.reshape(depth, A_HEADS // 2, PAIR, 2 * PAIR)
    bri = row(jnp.concatenate([a_b_r, a_b_i], axis=1))
    cols = lambda a, i: jnp.pad(a, ((0, 0), (0, 0), (i * B_WIDTH, (2 - i) * B_WIDTH)))
    wl = jnp.concatenate([cols(b_w2, 0), cols(b_a2, 1), cols(b_g2, 2)], axis=1).astype(BF16)
    shared = [a_conv_w, row(a_conv_b), wri, bri, row(a_lambda), row(b_mu), wl, row(b_w0), row(b_a0),
              row(b_k_k), row(b_k_a), row(b_r_k), row(b_ln_g), row(b_ln_b)]
    vmix = [row(b_v0), b_v1.astype(BF16), b_v2.astype(BF16)]
    tail = [row(lb_all), row(jnp.tile(c_norm, (1, C_HEADS)))]
    g_mix, g_ffn = row(mix_norm), row(ffn_norm)
    w_up_b, w_down_b = w_up.astype(BF16), w_down.astype(BF16)

    v_first = None
    for l in range(depth):
        if l == 0:
            proj, h = _proj_first(x, meta.astype(x.dtype), g_mix, w_in, lp)
            h = h.reshape(bsz * lp, d)
        else:
            proj = _proj(h, g_mix, w_in, l).reshape(bsz, lp, N_IN)
        params = [(a, l) for a in shared] + ([(a, l - 1) for a in vmix] if l else []) + [(a, l) for a in tail]
        y, v_first = _mixer(proj, v_first, params, consts, l == 0)
        if l < depth - 1:
            h = _post(h, y.reshape(bsz * lp, D_MIX), w_out, g_ffn, w_up_b, w_down_b, l)
    return _post_last(h.reshape(bsz, lp, d), y, w_out, g_ffn, w_up_b, w_down_b,
                      final_norm.reshape(1, -1), depth - 1, seq)
```

```python
import functools

import numpy as np
import jax
import jax.numpy as jnp
from jax import lax
from jax.experimental import pallas as pl
from jax.experimental.pallas import tpu as pltpu

F32 = jnp.float32
BF16 = jnp.bfloat16

N_META = 16
NORM_EPS = 1e-6
CHUNK = 64
SUB = 3
NBATCH = 4
STEP = SUB * CHUNK

A_WIDTH = 384
A_HEADS = 6
A_C = 8.0
B_WIDTH = 384
B_HEAD = 64
B_HEADS = 6
B_LORA = 128
B_COLS = 3 * B_WIDTH + B_LORA
B_GN_EPS = 64e-5
C_WIDTH = 256
C_HEADS = 4
C_HEAD = 64
C_LEVELS = 6
PAIR = 128
STAGGER = 4
A_OFF = 0
B_OFF = 2 * A_WIDTH
C_OFF = B_OFF + B_COLS
N_IN = C_OFF + 4 * C_WIDTH
D_MIX = A_WIDTH + B_WIDTH + C_WIDTH

V7X_VMEM_LIMIT = 56 * 1024 * 1024
CAST_SLABS = 16
F_CHUNK = 1024
TINY = 1e-30
KK_NORM_FLOOR = 1e-12
ROW_TILE = 704


def _dot(a, b, ca=1, cb=0):
    dims = (((ca,), (cb,)), ((), ()))
    return lax.dot_general(a.astype(BF16), b.astype(BF16), dims, preferred_element_type=F32)


def _split2(x):
    hi = x.astype(BF16)
    return hi, (x - hi.astype(F32)).astype(BF16)


def _sel_left(m, x):
    n = x.shape[1]
    r = jnp.dot(m, jnp.concatenate(_split2(x), axis=1), preferred_element_type=F32)
    return r[:, :n] + r[:, n:]


def _bd(y, first):
    return jnp.concatenate([jnp.where(first, y, 0.0), jnp.where(first, 0.0, y)], axis=0)


def _head_sums(x, bo):
    npair = x.shape[1] // PAIR
    rows = x.shape[0]
    r = _dot(jnp.concatenate([x[:, p * PAIR:(p + 1) * PAIR] for p in range(npair)], axis=0), bo)
    return jnp.concatenate([r[p * rows:(p + 1) * rows] for p in range(npair)], axis=1)


def _sigmoid(x):
    return jax.nn.sigmoid(x)


def _softplus(x):
    return jnp.maximum(x, 0.0) + jnp.log1p(jnp.exp(-jnp.abs(x)))


def _rmsnorm(x, g):
    return x * lax.rsqrt(jnp.mean(x * x, axis=-1, keepdims=True) + NORM_EPS) * g


def _proj_kernel(h_ref, g_ref, w_ref, wu_ref, wd_ref, o_ref, wub_ref, wdb_ref, wb):
    @pl.when(pl.program_id(0) == 0)
    def _():
        wb[...] = w_ref[...].astype(BF16)

    wub_ref[...] = wu_ref[...].astype(BF16)
    wdb_ref[...] = wd_ref[...].astype(BF16)
    xn = _rmsnorm(h_ref[...], g_ref[...])
    o_ref[...] = jnp.dot(xn.astype(BF16), wb[...], preferred_element_type=F32)


def _proj_first_kernel(l_real, x_ref, xp_ref, meta_ref, g_ref, w_ref, wu_ref, wd_ref,
                       o_ref, h_ref, wub_ref, wdb_ref, wb):
    j = pl.program_id(1)

    @pl.when((pl.program_id(0) == 0) & (j == 0))
    def _():
        wb[...] = w_ref[...].astype(BF16)

    wub_ref[...] = wu_ref[...].astype(BF16)
    wdb_ref[...] = wd_ref[...].astype(BF16)
    top = jnp.where(j == 0, meta_ref[...], xp_ref[...])
    h = jnp.concatenate([top, x_ref[0:ROW_TILE - N_META, :]], axis=0)
    t = j * ROW_TILE + lax.broadcasted_iota(jnp.int32, h.shape, 0)
    h = jnp.where(t < l_real, h, 0.0)
    h_ref[...] = h
    xn = _rmsnorm(h, g_ref[...])
    o_ref[...] = jnp.dot(xn.astype(BF16), wb[...], preferred_element_type=F32)


def _proj_first(x, meta, g, w, w_up, w_down, lp):
    bsz, seq, d = x.shape
    n = w.shape[2]
    t = ROW_TILE
    per = t // N_META
    last_main = -(-seq // t) - 1
    last_prev = seq // N_META - 1
    assert lp // t == last_main + 1, "every tile of the padded length must overlap x"
    assert bsz * (lp // t) >= CAST_SLABS
    s_in, s_out, s_shape = _slab_specs(w_up, w_down, 0, lambda b, j: b * (lp // t) + j)
    return pl.pallas_call(
        functools.partial(_proj_first_kernel, N_META + seq),
        grid=(bsz, lp // t),
        in_specs=[pl.BlockSpec((None, t, d), lambda b, j: (b, jnp.minimum(j, last_main), 0)),
                  pl.BlockSpec((None, N_META, d),
                               lambda b, j: (b, jnp.clip(j * per - 1, 0, last_prev), 0)),
                  pl.BlockSpec(meta.shape, lambda b, j: (0, 0)),
                  _layer_spec(g, 0), _layer_spec(w, 0, single=True)] + s_in,
        out_specs=[pl.BlockSpec((None, t, n), lambda b, j: (b, j, 0)),
                   pl.BlockSpec((None, t, d), lambda b, j: (b, j, 0))] + s_out,
        out_shape=[jax.ShapeDtypeStruct((bsz, lp, n), F32), jax.ShapeDtypeStruct((bsz, lp, d), F32)] + s_shape,
        scratch_shapes=[pltpu.VMEM((d, n), BF16)],
        compiler_params=pltpu.CompilerParams(dimension_semantics=("arbitrary", "arbitrary"),
                                             vmem_limit_bytes=V7X_VMEM_LIMIT),
        name="proj_first",
    )(x, x, meta, g, w, w_up, w_down)


def _slab_specs(w_up, w_down, layer, step):
    (_, d, d_ff) = w_up.shape
    ru, rd = d // CAST_SLABS, d_ff // CAST_SLABS
    slab = lambda *idx: jnp.minimum(step(*idx), CAST_SLABS - 1)
    ins = [pl.BlockSpec((None, ru, d_ff), lambda *idx: (layer, slab(*idx), 0)),
           pl.BlockSpec((None, rd, d), lambda *idx: (layer, slab(*idx), 0))]
    outs = [pl.BlockSpec((ru, d_ff), lambda *idx: (slab(*idx), 0)),
            pl.BlockSpec((rd, d), lambda *idx: (slab(*idx), 0))]
    shapes = [jax.ShapeDtypeStruct((d, d_ff), BF16), jax.ShapeDtypeStruct((d_ff, d), BF16)]
    return ins, outs, shapes


def _layer_spec(arr, layer, single=False):
    zeros = (0,) * (arr.ndim - 1)
    kw = dict(pipeline_mode=pl.Buffered(1)) if single else {}
    return pl.BlockSpec((None,) + arr.shape[1:], lambda *_: (layer,) + zeros, **kw)


def _whole_spec(arr):
    zeros = (0,) * arr.ndim
    return pl.BlockSpec(arr.shape, lambda *_: zeros, pipeline_mode=pl.Buffered(1))


def _proj(h2d, g, w, w_up, w_down, layer):
    rows, d = h2d.shape
    n = w.shape[2]
    assert rows // ROW_TILE >= CAST_SLABS
    s_in, s_out, s_shape = _slab_specs(w_up, w_down, layer, lambda i: i)
    return pl.pallas_call(
        _proj_kernel,
        grid=(rows // ROW_TILE,),
        in_specs=[pl.BlockSpec((ROW_TILE, d), lambda i: (i, 0)),
                  _layer_spec(g, layer), _layer_spec(w, layer, single=True)] + s_in,
        out_specs=[pl.BlockSpec((ROW_TILE, n), lambda i: (i, 0))] + s_out,
        out_shape=[jax.ShapeDtypeStruct((rows, n), F32)] + s_shape,
        scratch_shapes=[pltpu.VMEM((d, n), BF16)],
        compiler_params=pltpu.CompilerParams(dimension_semantics=("arbitrary",),
                                             vmem_limit_bytes=V7X_VMEM_LIMIT),
        name="proj",
    )(h2d, g, w, w_up, w_down)


def _residual_mlp(hs, ys, wob, g_ref, wu_ref, wd_ref, f_chunk):
    h1 = hs + jnp.dot(ys.astype(BF16), wob[...], preferred_element_type=F32)
    hb = _rmsnorm(h1, g_ref[...]).astype(BF16)
    acc = h1
    d_ff = wu_ref.shape[1]
    for c in range(d_ff // f_chunk):
        u = jnp.dot(hb, wu_ref[:, c * f_chunk:(c + 1) * f_chunk], preferred_element_type=F32)
        a = jnp.square(jnp.maximum(u, 0.0)).astype(BF16)
        acc = acc + jnp.dot(a, wd_ref[c * f_chunk:(c + 1) * f_chunk, :],
                            preferred_element_type=F32)
    return acc


def _post_kernel(f_chunk, h_ref, y_ref, wo_ref, g_ref, wu_ref, wd_ref, o_ref, wob):
    @pl.when(pl.program_id(0) == 0)
    def _():
        wob[...] = wo_ref[...].astype(BF16)

    o_ref[...] = _residual_mlp(h_ref[...], y_ref[...], wob, g_ref, wu_ref, wd_ref, f_chunk)


def _post(h2d, y2d, wo, g, wu, wd, layer):
    rows, d = h2d.shape
    tile = ROW_TILE
    return pl.pallas_call(
        functools.partial(_post_kernel, F_CHUNK),
        grid=(rows // tile,),
        in_specs=[pl.BlockSpec((tile, d), lambda i: (i, 0)),
                  pl.BlockSpec((tile, D_MIX), lambda i: (i, 0)),
                  _layer_spec(wo, layer, single=True), _layer_spec(g, layer),
                  _whole_spec(wu), _whole_spec(wd)],
        out_specs=pl.BlockSpec((tile, d), lambda i: (i, 0)),
        out_shape=jax.ShapeDtypeStruct((rows, d), F32),
        scratch_shapes=[pltpu.VMEM((D_MIX, d), BF16)],
        compiler_params=pltpu.CompilerParams(dimension_semantics=("arbitrary",),
                                             vmem_limit_bytes=V7X_VMEM_LIMIT),
        name="post",
    )(h2d, y2d, wo, g, wu, wd)


def _post_last_kernel(f_chunk, h_ref, hx_ref, y_ref, yx_ref, wo_ref, g_ref, wu_ref, wd_ref, gf_ref, o_ref, wob):
    @pl.when((pl.program_id(0) == 0) & (pl.program_id(1) == 0))
    def _():
        wob[...] = wo_ref[...].astype(BF16)

    shift = lambda main, extra: jnp.concatenate([main[N_META:, :], extra[...]], axis=0)
    acc = _residual_mlp(shift(h_ref, hx_ref), shift(y_ref, yx_ref), wob, g_ref, wu_ref, wd_ref, f_chunk)
    o_ref[...] = _rmsnorm(acc, gf_ref[...])


def _post_last(h3d, y3d, wo, g, wu, wd, gf, layer, seq):
    bsz, lp, d = h3d.shape
    t = ROW_TILE
    per = t // N_META
    last_extra = lp // N_META - 1
    assert -(-seq // t) == lp // t, "every tile of the padded length must overlap the output"
    main = lambda b, j: (b, j, 0)
    extra = lambda b, j: (b, jnp.minimum((j + 1) * per, last_extra), 0)
    return pl.pallas_call(
        functools.partial(_post_last_kernel, F_CHUNK),
        grid=(bsz, lp // t),
        in_specs=[pl.BlockSpec((None, t, d), main), pl.BlockSpec((None, N_META, d), extra),
                  pl.BlockSpec((None, t, D_MIX), main), pl.BlockSpec((None, N_META, D_MIX), extra),
                  _layer_spec(wo, layer, single=True), _layer_spec(g, layer),
                  _whole_spec(wu), _whole_spec(wd),
                  pl.BlockSpec(gf.shape, lambda b, j: (0, 0))],
        out_specs=pl.BlockSpec((None, t, d), main),
        out_shape=jax.ShapeDtypeStruct((bsz, seq, d), F32),
        scratch_shapes=[pltpu.VMEM((D_MIX, d), BF16)],
        compiler_params=pltpu.CompilerParams(dimension_semantics=("arbitrary", "arbitrary"),
                                             vmem_limit_bytes=V7X_VMEM_LIMIT),
        name="post_last",
    )(h3d, h3d, y3d, y3d, wo, g, wu, wd, gf)


def _rows(j):
    return slice(j * CHUNK, (j + 1) * CHUNK)


def _last_rows(x):
    return jnp.concatenate([jnp.broadcast_to(x[(j + 1) * CHUNK - 1:(j + 1) * CHUNK, :], (CHUNK, x.shape[1]))
                            for j in range(SUB)], axis=0)


def _level_decays(b):
    rows, n = b.shape
    out = []
    sub = lax.broadcasted_iota(jnp.int32, (rows, n), 0) & 7
    b3 = b.reshape(rows // 8, 8, n)
    pick = lambda k: jnp.broadcast_to(b3[:, k:k + 1, :], b3.shape).reshape(rows, n)
    for l in range(C_LEVELS):
        half = 1 << l
        if l == 0:
            m = jnp.where((sub & 1) == 1, pltpu.roll(b, 1, 0), b)
        elif l == 1:
            m = jnp.where(sub < 4, pick(1), pick(5))
        elif l == 2:
            m = pick(3)
        else:
            blk = 2 * half
            m = jnp.concatenate([jnp.broadcast_to(b[i * blk + half - 1:i * blk + half, :], (blk, n))
                                 for i in range(rows // blk)], axis=0)
        out.append(-jnp.abs(b - m))
    return out


def _group_a(bi, p_ref, o_ref, cw_ref, cb_ref, wri_ref, bri_ref, lam_ref, xbuf, h_a):
    t = STEP
    xa = p_ref[bi, :, A_OFF:A_OFF + A_WIDTH]
    ga = p_ref[bi, :, A_OFF + A_WIDTH:A_OFF + 2 * A_WIDTH]
    xbuf[bi, 8:8 + t, :] = xa
    u = cb_ref[...]
    for j in range(4):
        u = u + cw_ref[j:j + 1, :] * xbuf[bi, 5 + j:5 + j + t, :]
    xbuf[bi, 0:8, :] = xbuf[bi, t:t + 8, :]
    yield
    ri = [_dot(u[:, p * PAIR:(p + 1) * PAIR], wri_ref[p]) for p in range(A_WIDTH // PAIR)]
    r = _sigmoid(jnp.concatenate([x[:, :PAIR] for x in ri], axis=1) + bri_ref[:, :A_WIDTH])
    i = _sigmoid(jnp.concatenate([x[:, PAIR:] for x in ri], axis=1) + bri_ref[:, A_WIDTH:])
    yield
    log_a = (-A_C * r) * _softplus(-lam_ref[...])
    a = jnp.exp(log_a)
    om = 1.0 - a * a
    d = (om * lax.rsqrt(jnp.maximum(om, TINY))) * (i * u)
    yield
    rows = lax.broadcasted_iota(jnp.int32, (t, A_WIDTH), 0) & 7
    for s in (1, 2, 4):
        a_sh = jnp.where(rows < s, 1.0, pltpu.roll(a, s, 0))
        d_sh = jnp.where(rows < s, 0.0, pltpu.roll(d, s, 0))
        d = a * d_sh + d
        a = a * a_sh
        yield
    carry = h_a[bi, 0:1, :]
    hs = []
    for g in range(t // 8):
        hg = d[8 * g:8 * g + 8, :] + a[8 * g:8 * g + 8, :] * carry
        carry = hg[7:8, :]
        hs.append(hg)
        if g % 8 == 7:
            yield
    hseq = jnp.concatenate(hs, axis=0)
    h_a[bi, 0:1, :] = carry
    o_ref[bi, :, 0:A_WIDTH] = hseq * jax.nn.gelu(ga, approximate=True)


def _group_c(bi, p_ref, o_ref, lb_ref, cn_ref, tril, bo, bdm, s_c):
    t = CHUNK
    q_raw = p_ref[bi, :, C_OFF:C_OFF + C_WIDTH]
    fz = p_ref[bi, :, C_OFF + C_WIDTH:C_OFF + 2 * C_WIDTH]
    v = p_ref[bi, :, C_OFF + 2 * C_WIDTH:C_OFF + 3 * C_WIDTH]
    og = p_ref[bi, :, C_OFF + 3 * C_WIDTH:C_OFF + 4 * C_WIDTH]
    lb = lb_ref[...]
    f = lb + (1.0 - lb) * _sigmoid(fz)
    g = jnp.log(f)
    kd = 1.0 - f
    q = q_raw * _sigmoid(q_raw)
    yield
    b = _sel_left(tril, g)
    e_lev = [jnp.exp(x) for x in _level_decays(b)]
    b_last = _last_rows(b)
    qe = q * jnp.exp(b)
    kl = kd * jnp.exp(b_last - b)
    g_last = jnp.exp(b_last)
    dq = _head_sums(q * kd, bo)
    yield
    ti = lax.broadcasted_iota(jnp.int32, (t, PAIR), 0)
    lane = lax.broadcasted_iota(jnp.int32, (t, PAIR), 1)
    first = lane < C_HEAD
    sj = lane & (C_HEAD - 1)
    xs = jnp.where(ti > sj, ti ^ sj, 0)
    npair = C_WIDTH // PAIR
    cs = [(j, p) for j in range(SUB) for p in range(npair)]
    sl = lambda x, c: x[_rows(c[0]), c[1] * PAIR:(c[1] + 1) * PAIR]
    att = [jnp.zeros((t, PAIR), F32) for _ in cs]
    for l in range(C_LEVELS):
        el = e_lev[l]
        ql = q * el
        kl_l = kd * el
        lev = (xs >> l) == 1
        att = [att[i] + jnp.where(lev, _dot(sl(ql, c), _bd(sl(kl_l, c), first), 1, 1), 0.0)
               for i, c in enumerate(cs)]
        yield
    intra = [_dot(att[i], _bd(sl(v, c), first)) + sl(dq, c) * sl(v, c) for i, c in enumerate(cs)]
    upd = [jnp.where(bdm, _dot(sl(v, c), sl(kl, c), 0, 0), 0.0) for c in cs]
    yield
    sts = {}
    for p in range(npair):
        st = s_c[bi, p]
        for j in range(SUB):
            i = cs.index((j, p))
            sts[i] = st
            st = st * g_last[(j + 1) * t - 1:(j + 1) * t, p * PAIR:(p + 1) * PAIR] + upd[i]
        s_c[bi, p] = st
    outs = [intra[i] + _dot(sl(qe, c), sts[i], 1, 1) for i, c in enumerate(cs)]
    o = jnp.concatenate([jnp.concatenate([outs[cs.index((j, p))] for p in range(npair)], axis=1)
                         for j in range(SUB)], axis=0)
    yield
    ms = _head_sums(o * o, bo) * (1.0 / C_HEAD)
    o_ref[bi, :, A_WIDTH + B_WIDTH:D_MIX] = o * lax.rsqrt(ms + NORM_EPS) * cn_ref[...] * (og * _sigmoid(og))


def _group_b(bi, p_ref, o_ref, vf_ref, vfo_ref, mu_ref, wl_ref, w0_ref, a0_ref, kk_ref, ka_ref, rk_ref,
             lng_ref, lnb_ref, v0_ref, v1_ref, v2_ref, tril, bo, bdm, pbuf, s_b):
    t = CHUNK
    w = B_WIDTH
    pb = p_ref[bi, :, B_OFF:B_OFF + B_COLS]
    pbuf[bi, 8:8 + STEP, :] = pb
    prev = pbuf[bi, 7:7 + STEP, :]
    pbuf[bi, 0:8, :] = pbuf[bi, STEP:STEP + 8, :]
    sh = pb + mu_ref[...] * (prev - pb)
    r = sh[:, 0:w]
    k = sh[:, w:2 * w]
    v = sh[:, 2 * w:3 * w]
    lo = sh[:, 3 * w:3 * w + B_LORA]
    lane = lax.broadcasted_iota(jnp.int32, (STEP, B_LORA), 1)
    xl = jnp.where(lane < 32, jnp.tanh(lo), jnp.where(lane < 64, lo, _sigmoid(lo)))
    lora = _dot(xl, wl_ref[...])
    lw = -float(np.exp(-0.5)) * _sigmoid(w0_ref[...] + lora[:, 0:w])
    if vf_ref is None:
        vfo_ref[bi] = v
    else:
        mix = _sigmoid(v0_ref[...] + _dot(_dot(v, v1_ref[...]), v2_ref[...]))
        v = v + (vf_ref[bi] - v) * mix
    aa = _sigmoid(a0_ref[...] + lora[:, w:2 * w])
    gate = lora[:, 2 * w:3 * w]
    kk = k * kk_ref[...]
    k = k * (1.0 + (aa - 1.0) * ka_ref[...])
    sums = _head_sums(jnp.concatenate([kk * kk, r * k * rk_ref[...]], axis=0), bo)
    kk = kk * lax.rsqrt(jnp.maximum(sums[:STEP], KK_NORM_FLOOR ** 2))
    rk = sums[STEP:]
    ap = -kk
    bb = kk * aa
    c = _sel_left(tril, lw)
    c_last = _last_rows(c)
    a_h = ap * jnp.exp(c - lw)
    r_h = r * jnp.exp(c)
    e_inv = jnp.exp(-c)
    b_h = bb * e_inv
    k_h = k * e_inv
    e_cl = jnp.exp(c_last - c)
    b_c = bb * e_cl
    k_c = k * e_cl
    gam = jnp.exp(c_last)
    yield

    ti = lax.broadcasted_iota(jnp.int32, (t, PAIR), 0)
    lane = lax.broadcasted_iota(jnp.int32, (t, PAIR), 1)
    first = lane < B_HEAD
    sj = lane & (B_HEAD - 1)
    strict = ti > sj
    lower = ti >= sj
    blk = (ti >> 4) == (sj >> 4)
    eye = jnp.where(ti == sj, 1.0, 0.0)
    bd = lambda y: _bd(y, first)
    cat0 = lambda xs: jnp.concatenate(xs, axis=0)
    cat1 = lambda xs: jnp.concatenate(xs, axis=1)
    npair = B_WIDTH // PAIR
    cs = [(j, p) for j in range(SUB) for p in range(npair)]
    n = range(len(cs))
    sl = lambda x, i: x[_rows(cs[i][0]), cs[i][1] * PAIR:(cs[i][1] + 1) * PAIR]
    vbd = [bd(sl(v, i)) for i in n]
    ar = [cat0([sl(a_h, i), sl(r_h, i)]) for i in n]
    mm = [_dot(ar[i], cat0([bd(sl(b_h, i)), bd(sl(k_h, i))]), 1, 1) for i in n]
    yield
    m_ab = [jnp.where(strict, m[:t, :PAIR], 0.0) for m in mm]
    m_ak = [jnp.where(strict, m[:t, PAIR:], 0.0) for m in mm]
    n_rb = [jnp.where(lower, m[t:, :PAIR], 0.0) for m in mm]
    n_rk = [jnp.where(lower, m[t:, PAIR:], 0.0) for m in mm]
    dg = [jnp.where(blk, m, 0.0) for m in m_ab]
    off = [m_ab[i] - dg[i] for i in n]
    akv = [_dot(m_ak[i], vbd[i]) for i in n]
    nkv = [_dot(n_rk[i], vbd[i]) for i in n]
    d2 = [_dot(dg[i], bd(dg[i])) for i in n]
    yield
    s4 = [_dot(cat0([d2[i], dg[i]]), bd(d2[i])) for i in n]
    yield
    d4 = [x[:t] for x in s4]
    x1 = [eye + dg[i] + d2[i] + s4[i][t:] for i in n]
    s8 = [_dot(cat0([d4[i], x1[i]]), bd(d4[i])) for i in n]
    yield
    x2 = [x1[i] + s8[i][t:] for i in n]
    dinv = [x2[i] + _dot(x2[i], bd(s8[i][:t])) for i in n]
    yield
    gm = [_dot(dinv[i], bd(off[i])) for i in n]
    yield
    g2 = [_dot(gm[i], bd(gm[i])) for i in n]
    yield
    gs = [eye + gm[i] + g2[i] + _dot(gm[i], bd(g2[i])) for i in n]
    yield
    tinv = [_dot(gs[i], bd(dinv[i])) for i in n]
    yield
    pq = [_dot(tinv[i], cat1([bd(sl(a_h, i)), bd(akv[i])])) for i in n]
    yield
    npq = [_dot(n_rb[i], cat1([bd(pq[i][:, :PAIR]), bd(pq[i][:, PAIR:])])) for i in n]
    yield
    p2 = [sl(r_h, i) + npq[i][:, :PAIR] for i in n]
    q2 = [npq[i][:, PAIR:] + nkv[i] for i in n]
    wp = [jnp.where(bdm, _dot(pq[i][:, :PAIR], sl(b_c, i), 0, 0), 0.0) for i in n]
    zz = [jnp.where(bdm, _dot(cat0([pq[i][:, PAIR:], sl(v, i)]), cat0([sl(b_c, i), sl(k_c, i)]), 0, 0), 0.0)
          for i in n]
    yield
    sts = {}
    for p in range(npair):
        st = s_b[bi, p]
        for j in range(SUB):
            i = cs.index((j, p))
            sts[i] = st
            st = st * gam[(j + 1) * t - 1:(j + 1) * t, p * PAIR:(p + 1) * PAIR] + _dot(st, wp[i]) + zz[i]
        s_b[bi, p] = st
    ys = [_dot(p2[i], sts[i], 1, 1) + q2[i] for i in n]
    yield
    y = cat0([cat1([ys[cs.index((j, p))] for p in range(npair)]) for j in range(SUB)])
    inv_n = 1.0 / B_HEAD
    mean = _head_sums(y, bo) * inv_n
    yc = y - mean
    var = _head_sums(yc * yc, bo) * inv_n
    y = yc * lax.rsqrt(var + B_GN_EPS) * lng_ref[...] + lnb_ref[...]
    y = y + rk * v
    o_ref[bi, :, A_WIDTH:A_WIDTH + B_WIDTH] = y * gate


def _mixer_kernel(layer0, *refs):
    it = iter(refs)
    p_ref = next(it)
    vf_ref = None if layer0 else next(it)
    cw, cb, wri, bri, lam = (next(it) for _ in range(5))
    mu, wl, w0, a0, kkp, kap, rkp, lng, lnb = (next(it) for _ in range(9))
    v0 = v1 = v2 = None
    if not layer0:
        v0, v1, v2 = (next(it) for _ in range(3))
    lb, cn, tril_ref, bo_ref = (next(it) for _ in range(4))
    y_ref = next(it)
    vfo_ref = next(it) if layer0 else None
    xbuf, h_a, pbuf, s_b, s_c = (next(it) for _ in range(5))

    @pl.when(pl.program_id(1) == 0)
    def _():
        xbuf[:, 0:8, :] = jnp.zeros((NBATCH, 8, A_WIDTH), F32)
        pbuf[:, 0:8, :] = jnp.zeros((NBATCH, 8, B_COLS), F32)
        h_a[...] = jnp.zeros_like(h_a)
        s_b[...] = jnp.zeros_like(s_b)
        s_c[...] = jnp.zeros_like(s_c)

    tril = tril_ref[...]
    bo = bo_ref[...]
    half = lambda d: lax.broadcasted_iota(jnp.int32, (PAIR, PAIR), d) < PAIR // 2
    bdm = half(0) == half(1)
    live = []
    for bi in range(NBATCH):
        gb = _group_b(bi, p_ref, y_ref, vf_ref, vfo_ref, mu, wl, w0, a0, kkp, kap, rkp, lng, lnb, v0, v1, v2,
                      tril, bo, bdm, pbuf, s_b)
        gc = _group_c(bi, p_ref, y_ref, lb, cn, tril, bo, bdm, s_c)
        ga = _group_a(bi, p_ref, y_ref, cw, cb, wri, bri, lam, xbuf, h_a)
        next(gb)
        live += [gb, gc, ga]
        rounds = STAGGER if bi + 1 < NBATCH else 1 << 30
        while live and rounds:
            rounds -= 1
            for gen in list(live):
                try:
                    next(gen)
                except StopIteration:
                    live.remove(gen)


def _step_constants():
    eye = np.eye(SUB, dtype=np.float32)
    tril = np.kron(eye, np.tril(np.ones((CHUNK, CHUNK), np.float32)))
    hid = np.arange(PAIR) // B_HEAD
    bo = (hid[:, None] == hid[None, :]).astype(np.float32)
    return tuple(jnp.asarray(m, BF16) for m in (tril, bo))


def _mixer(proj, v_first, params, consts, layer0):
    bsz, lp, _ = proj.shape
    t = STEP
    const = lambda b, c: (0, 0)
    row = lambda b, c: (b, c, 0)
    ins = [proj]
    specs = [pl.BlockSpec((NBATCH, t, N_IN), row)]
    if not layer0:
        ins.append(v_first)
        specs.append(pl.BlockSpec((NBATCH, t, B_WIDTH), row))
    for arr, layer in params:
        ins.append(arr)
        specs.append(_layer_spec(arr, layer))
    for arr in consts:
        ins.append(arr)
        specs.append(pl.BlockSpec(arr.shape, const))
    out_shape = [jax.ShapeDtypeStruct((bsz, lp, D_MIX), F32)]
    out_specs = [pl.BlockSpec((NBATCH, t, D_MIX), row)]
    if layer0:
        out_shape.append(jax.ShapeDtypeStruct((bsz, lp, B_WIDTH), F32))
        out_specs.append(pl.BlockSpec((NBATCH, t, B_WIDTH), row))
    res = pl.pallas_call(
        functools.partial(_mixer_kernel, layer0),
        grid=(bsz // NBATCH, lp // t),
        in_specs=specs,
        out_specs=out_specs,
        out_shape=out_shape,
        scratch_shapes=[pltpu.VMEM((NBATCH, t + 8, A_WIDTH), F32),
                        pltpu.VMEM((NBATCH, 8, A_WIDTH), F32),
                        pltpu.VMEM((NBATCH, t + 8, B_COLS), F32),
                        pltpu.VMEM((NBATCH, B_WIDTH // PAIR, PAIR, PAIR), F32),
                        pltpu.VMEM((NBATCH, C_WIDTH // PAIR, PAIR, PAIR), F32)],
        compiler_params=pltpu.CompilerParams(dimension_semantics=("arbitrary", "arbitrary"),
                                             vmem_limit_bytes=V7X_VMEM_LIMIT),
        name="mixer0" if layer0 else "mixer",
    )(*ins)
    return res if layer0 else (res[0], v_first)


def _block_diag(w):
    nl, hn, n, _ = w.shape
    return jnp.concatenate([jnp.pad(w[:, h], ((0, 0), (0, 0), (h * n, (hn - 1 - h) * n))) for h in range(hn)],
                           axis=1)


def kernel(x, meta, mix_norm, w_in, a_conv_w, a_conv_b, a_w_r, a_b_r, a_w_i, a_b_i, a_lambda, b_mu, b_w0, b_w2, b_a0, b_a2, b_g2, b_k_k, b_k_a, b_r_k, b_ln_g, b_ln_b, b_v0, b_v1, b_v2, c_lb, c_norm, w_out, ffn_norm, w_up, w_down, final_norm):
    bsz, seq, d = x.shape
    depth = w_in.shape[0]
    l_real = N_META + seq
    lp = -(-l_real // STEP) * STEP
    while (bsz * lp) % ROW_TILE:
        lp += STEP
    lb_all = jnp.cumsum(jax.nn.softmax(c_lb.astype(F32), axis=0), axis=0)
    lb_all = lb_all - lb_all[0]
    consts = _step_constants()
    row = lambda a: a.reshape(a.shape[0], 1, -1)
    pair_bd = lambda w: _block_diag(w.reshape(depth * (A_HEADS // 2), 2, *w.shape[2:]))
    wri = jnp.concatenate([pair_bd(a_w_r), pair_bd(a_w_i)], axis=2).astype(BF16)
    wri = wri.reshape(depth, A_HEADS // 2, PAIR, 2 * PAIR)
    bri = row(jnp.concatenate([a_b_r, a_b_i], axis=1))
    cols = lambda a, i: jnp.pad(a, ((0, 0), (0, 0), (i * B_WIDTH, (2 - i) * B_WIDTH)))
    wl = jnp.concatenate([cols(b_w2, 0), cols(b_a2, 1), cols(b_g2, 2)], axis=1).astype(BF16)
    shared = [a_conv_w, row(a_conv_b), wri, bri, row(a_lambda), row(b_mu), wl, row(b_w0), row(b_a0),
              row(b_k_k), row(b_k_a), row(b_r_k), row(b_ln_g), row(b_ln_b)]
    vmix = [row(b_v0), b_v1.astype(BF16), b_v2.astype(BF16)]
    tail = [row(lb_all), row(jnp.tile(c_norm, (1, C_HEADS)))]
    g_mix, g_ffn = row(mix_norm), row(ffn_norm)

    v_first = None
    for l in range(depth):
        if l == 0:
            proj, h, w_up_b, w_down_b = _proj_first(x, meta.astype(x.dtype), g_mix, w_in, w_up, w_down, lp)
            h = h.reshape(bsz * lp, d)
        else:
            proj, w_up_b, w_down_b = _proj(h, g_mix, w_in, w_up, w_down, l)
            proj = proj.reshape(bsz, lp, N_IN)
        params = [(a, l) for a in shared] + ([(a, l - 1) for a in vmix] if l else []) + [(a, l) for a in tail]
        y, v_first = _mixer(proj, v_first, params, consts, l == 0)
        if l < depth - 1:
            h = _post(h, y.reshape(bsz * lp, D_MIX), w_out, g_ffn, w_up_b, w_down_b, l)
    return _post_last(h.reshape(bsz, lp, d), y, w_out, g_ffn, w_up_b, w_down_b,
                      final_norm.reshape(1, -1), depth - 1, seq)
```

```python
import functools

import numpy as np
import jax
import jax.numpy as jnp
from jax import lax
from jax.experimental import pallas as pl
from jax.experimental.pallas import tpu as pltpu

F32 = jnp.float32
BF16 = jnp.bfloat16

N_META = 16
NORM_EPS = 1e-6
CHUNK = 64
SUB = 3
NBATCH = 4
STEP = SUB * CHUNK

A_WIDTH = 384
A_HEADS = 6
A_C = 8.0
B_WIDTH = 384
B_HEAD = 64
B_HEADS = 6
B_LORA = 128
B_COLS = 3 * B_WIDTH + B_LORA
B_GN_EPS = 64e-5
C_WIDTH = 256
C_HEADS = 4
C_HEAD = 64
C_LEVELS = 6
PAIR = 128
STAGGER = 4
A_OFF = 0
B_OFF = 2 * A_WIDTH
C_OFF = B_OFF + B_COLS
N_IN = C_OFF + 4 * C_WIDTH
D_MIX = A_WIDTH + B_WIDTH + C_WIDTH

V7X_VMEM_LIMIT = 56 * 1024 * 1024
CAST_SLABS = 16
F_CHUNK = 2048
TINY = 1e-30
KK_NORM_FLOOR = 1e-12
ROW_TILE = 704


def _dot(a, b, ca=1, cb=0):
    dims = (((ca,), (cb,)), ((), ()))
    return lax.dot_general(a.astype(BF16), b.astype(BF16), dims, preferred_element_type=F32)


def _split2(x):
    hi = x.astype(BF16)
    return hi, (x - hi.astype(F32)).astype(BF16)


def _sel_left(m, x):
    n = x.shape[1]
    r = jnp.dot(m, jnp.concatenate(_split2(x), axis=1), preferred_element_type=F32)
    return r[:, :n] + r[:, n:]


def _bd(y, first):
    return jnp.concatenate([jnp.where(first, y, 0.0), jnp.where(first, 0.0, y)], axis=0)


def _head_sums(x, bo):
    npair = x.shape[1] // PAIR
    rows = x.shape[0]
    r = _dot(jnp.concatenate([x[:, p * PAIR:(p + 1) * PAIR] for p in range(npair)], axis=0), bo)
    return jnp.concatenate([r[p * rows:(p + 1) * rows] for p in range(npair)], axis=1)


def _sigmoid(x):
    return jax.nn.sigmoid(x)


def _softplus(x):
    return jnp.maximum(x, 0.0) + jnp.log1p(jnp.exp(-jnp.abs(x)))


def _rmsnorm(x, g):
    return x * lax.rsqrt(jnp.mean(x * x, axis=-1, keepdims=True) + NORM_EPS) * g


def _proj_kernel(h_ref, g_ref, w_ref, wu_ref, wd_ref, o_ref, wub_ref, wdb_ref, wb):
    @pl.when(pl.program_id(0) == 0)
    def _():
        wb[...] = w_ref[...].astype(BF16)

    wub_ref[...] = wu_ref[...].astype(BF16)
    wdb_ref[...] = wd_ref[...].astype(BF16)
    xn = _rmsnorm(h_ref[...], g_ref[...])
    o_ref[...] = jnp.dot(xn.astype(BF16), wb[...], preferred_element_type=F32)


def _proj_first_kernel(l_real, x_ref, xp_ref, meta_ref, g_ref, w_ref, wu_ref, wd_ref,
                       o_ref, h_ref, wub_ref, wdb_ref, wb):
    j = pl.program_id(1)

    @pl.when((pl.program_id(0) == 0) & (j == 0))
    def _():
        wb[...] = w_ref[...].astype(BF16)

    wub_ref[...] = wu_ref[...].astype(BF16)
    wdb_ref[...] = wd_ref[...].astype(BF16)
    top = jnp.where(j == 0, meta_ref[...], xp_ref[...])
    h = jnp.concatenate([top, x_ref[0:ROW_TILE - N_META, :]], axis=0)
    t = j * ROW_TILE + lax.broadcasted_iota(jnp.int32, h.shape, 0)
    h = jnp.where(t < l_real, h, 0.0)
    h_ref[...] = h
    xn = _rmsnorm(h, g_ref[...])
    o_ref[...] = jnp.dot(xn.astype(BF16), wb[...], preferred_element_type=F32)


def _proj_first(x, meta, g, w, w_up, w_down, lp):
    bsz, seq, d = x.shape
    n = w.shape[2]
    t = ROW_TILE
    per = t // N_META
    last_main = -(-seq // t) - 1
    last_prev = seq // N_META - 1
    assert lp // t == last_main + 1, "every tile of the padded length must overlap x"
    assert bsz * (lp // t) >= CAST_SLABS
    s_in, s_out, s_shape = _slab_specs(w_up, w_down, 0, lambda b, j: b * (lp // t) + j)
    return pl.pallas_call(
        functools.partial(_proj_first_kernel, N_META + seq),
        grid=(bsz, lp // t),
        in_specs=[pl.BlockSpec((None, t, d), lambda b, j: (b, jnp.minimum(j, last_main), 0)),
                  pl.BlockSpec((None, N_META, d),
                               lambda b, j: (b, jnp.clip(j * per - 1, 0, last_prev), 0)),
                  pl.BlockSpec(meta.shape, lambda b, j: (0, 0)),
                  _layer_spec(g, 0), _layer_spec(w, 0, single=True)] + s_in,
        out_specs=[pl.BlockSpec((None, t, n), lambda b, j: (b, j, 0)),
                   pl.BlockSpec((None, t, d), lambda b, j: (b, j, 0))] + s_out,
        out_shape=[jax.ShapeDtypeStruct((bsz, lp, n), F32), jax.ShapeDtypeStruct((bsz, lp, d), F32)] + s_shape,
        scratch_shapes=[pltpu.VMEM((d, n), BF16)],
        compiler_params=pltpu.CompilerParams(dimension_semantics=("arbitrary", "arbitrary"),
                                             vmem_limit_bytes=V7X_VMEM_LIMIT),
        name="proj_first",
    )(x, x, meta, g, w, w_up, w_down)


def _slab_specs(w_up, w_down, layer, step):
    (_, d, d_ff) = w_up.shape
    ru, rd = d // CAST_SLABS, d_ff // CAST_SLABS
    slab = lambda *idx: jnp.minimum(step(*idx), CAST_SLABS - 1)
    ins = [pl.BlockSpec((None, ru, d_ff), lambda *idx: (layer, slab(*idx), 0)),
           pl.BlockSpec((None, rd, d), lambda *idx: (layer, slab(*idx), 0))]
    outs = [pl.BlockSpec((ru, d_ff), lambda *idx: (slab(*idx), 0)),
            pl.BlockSpec((rd, d), lambda *idx: (slab(*idx), 0))]
    shapes = [jax.ShapeDtypeStruct((d, d_ff), BF16), jax.ShapeDtypeStruct((d_ff, d), BF16)]
    return ins, outs, shapes


def _layer_spec(arr, layer, single=False):
    zeros = (0,) * (arr.ndim - 1)
    kw = dict(pipeline_mode=pl.Buffered(1)) if single else {}
    return pl.BlockSpec((None,) + arr.shape[1:], lambda *_: (layer,) + zeros, **kw)


def _whole_spec(arr):
    zeros = (0,) * arr.ndim
    return pl.BlockSpec(arr.shape, lambda *_: zeros, pipeline_mode=pl.Buffered(1))


def _proj(h2d, g, w, w_up, w_down, layer):
    rows, d = h2d.shape
    n = w.shape[2]
    assert rows // ROW_TILE >= CAST_SLABS
    s_in, s_out, s_shape = _slab_specs(w_up, w_down, layer, lambda i: i)
    return pl.pallas_call(
        _proj_kernel,
        grid=(rows // ROW_TILE,),
        in_specs=[pl.BlockSpec((ROW_TILE, d), lambda i: (i, 0)),
                  _layer_spec(g, layer), _layer_spec(w, layer, single=True)] + s_in,
        out_specs=[pl.BlockSpec((ROW_TILE, n), lambda i: (i, 0))] + s_out,
        out_shape=[jax.ShapeDtypeStruct((rows, n), F32)] + s_shape,
        scratch_shapes=[pltpu.VMEM((d, n), BF16)],
        compiler_params=pltpu.CompilerParams(dimension_semantics=("arbitrary",),
                                             vmem_limit_bytes=V7X_VMEM_LIMIT),
        name="proj",
    )(h2d, g, w, w_up, w_down)


def _residual_mlp(hs, ys, wob, g_ref, wu_ref, wd_ref, f_chunk):
    h1 = hs + jnp.dot(ys.astype(BF16), wob[...], preferred_element_type=F32)
    hb = _rmsnorm(h1, g_ref[...]).astype(BF16)
    acc = h1
    d_ff = wu_ref.shape[1]
    for c in range(d_ff // f_chunk):
        u = jnp.dot(hb, wu_ref[:, c * f_chunk:(c + 1) * f_chunk], preferred_element_type=F32)
        a = jnp.square(jnp.maximum(u, 0.0)).astype(BF16)
        acc = acc + jnp.dot(a, wd_ref[c * f_chunk:(c + 1) * f_chunk, :],
                            preferred_element_type=F32)
    return acc


def _post_kernel(f_chunk, h_ref, y_ref, wo_ref, g_ref, wu_ref, wd_ref, o_ref, wob):
    @pl.when(pl.program_id(0) == 0)
    def _():
        wob[...] = wo_ref[...].astype(BF16)

    o_ref[...] = _residual_mlp(h_ref[...], y_ref[...], wob, g_ref, wu_ref, wd_ref, f_chunk)


def _post(h2d, y2d, wo, g, wu, wd, layer):
    rows, d = h2d.shape
    tile = ROW_TILE
    return pl.pallas_call(
        functools.partial(_post_kernel, F_CHUNK),
        grid=(rows // tile,),
        in_specs=[pl.BlockSpec((tile, d), lambda i: (i, 0)),
                  pl.BlockSpec((tile, D_MIX), lambda i: (i, 0)),
                  _layer_spec(wo, layer, single=True), _layer_spec(g, layer),
                  _whole_spec(wu), _whole_spec(wd)],
        out_specs=pl.BlockSpec((tile, d), lambda i: (i, 0)),
        out_shape=jax.ShapeDtypeStruct((rows, d), F32),
        scratch_shapes=[pltpu.VMEM((D_MIX, d), BF16)],
        compiler_params=pltpu.CompilerParams(dimension_semantics=("arbitrary",),
                                             vmem_limit_bytes=V7X_VMEM_LIMIT),
        name="post",
    )(h2d, y2d, wo, g, wu, wd)


def _post_last_kernel(f_chunk, h_ref, hx_ref, y_ref, yx_ref, wo_ref, g_ref, wu_ref, wd_ref, gf_ref, o_ref, wob):
    @pl.when((pl.program_id(0) == 0) & (pl.program_id(1) == 0))
    def _():
        wob[...] = wo_ref[...].astype(BF16)

    shift = lambda main, extra: jnp.concatenate([main[N_META:, :], extra[...]], axis=0)
    acc = _residual_mlp(shift(h_ref, hx_ref), shift(y_ref, yx_ref), wob, g_ref, wu_ref, wd_ref, f_chunk)
    o_ref[...] = _rmsnorm(acc, gf_ref[...])


def _post_last(h3d, y3d, wo, g, wu, wd, gf, layer, seq):
    bsz, lp, d = h3d.shape
    t = ROW_TILE
    per = t // N_META
    last_extra = lp // N_META - 1
    assert -(-seq // t) == lp // t, "every tile of the padded length must overlap the output"
    main = lambda b, j: (b, j, 0)
    extra = lambda b, j: (b, jnp.minimum((j + 1) * per, last_extra), 0)
    return pl.pallas_call(
        functools.partial(_post_last_kernel, F_CHUNK),
        grid=(bsz, lp // t),
        in_specs=[pl.BlockSpec((None, t, d), main), pl.BlockSpec((None, N_META, d), extra),
                  pl.BlockSpec((None, t, D_MIX), main), pl.BlockSpec((None, N_META, D_MIX), extra),
                  _layer_spec(wo, layer, single=True), _layer_spec(g, layer),
                  _whole_spec(wu), _whole_spec(wd),
                  pl.BlockSpec(gf.shape, lambda b, j: (0, 0))],
        out_specs=pl.BlockSpec((None, t, d), main),
        out_shape=jax.ShapeDtypeStruct((bsz, seq, d), F32),
        scratch_shapes=[pltpu.VMEM((D_MIX, d), BF16)],
        compiler_params=pltpu.CompilerParams(dimension_semantics=("arbitrary", "arbitrary"),
                                             vmem_limit_bytes=V7X_VMEM_LIMIT),
        name="post_last",
    )(h3d, h3d, y3d, y3d, wo, g, wu, wd, gf)


def _rows(j):
    return slice(j * CHUNK, (j + 1) * CHUNK)


def _last_rows(x):
    return jnp.concatenate([jnp.broadcast_to(x[(j + 1) * CHUNK - 1:(j + 1) * CHUNK, :], (CHUNK, x.shape[1]))
                            for j in range(SUB)], axis=0)


def _level_decays(b):
    rows, n = b.shape
    out = []
    sub = lax.broadcasted_iota(jnp.int32, (rows, n), 0) & 7
    b3 = b.reshape(rows // 8, 8, n)
    pick = lambda k: jnp.broadcast_to(b3[:, k:k + 1, :], b3.shape).reshape(rows, n)
    for l in range(C_LEVELS):
        half = 1 << l
        if l == 0:
            m = jnp.where((sub & 1) == 1, pltpu.roll(b, 1, 0), b)
        elif l == 1:
            m = jnp.where(sub < 4, pick(1), pick(5))
        elif l == 2:
            m = pick(3)
        else:
            blk = 2 * half
            m = jnp.concatenate([jnp.broadcast_to(b[i * blk + half - 1:i * blk + half, :], (blk, n))
                                 for i in range(rows // blk)], axis=0)
        out.append(-jnp.abs(b - m))
    return out


def _group_a(bi, p_ref, o_ref, cw_ref, cb_ref, wri_ref, bri_ref, lam_ref, xbuf, h_a):
    t = STEP
    xa = p_ref[bi, :, A_OFF:A_OFF + A_WIDTH]
    ga = p_ref[bi, :, A_OFF + A_WIDTH:A_OFF + 2 * A_WIDTH]
    xbuf[bi, 8:8 + t, :] = xa
    u = cb_ref[...]
    for j in range(4):
        u = u + cw_ref[j:j + 1, :] * xbuf[bi, 5 + j:5 + j + t, :]
    xbuf[bi, 0:8, :] = xbuf[bi, t:t + 8, :]
    yield
    ri = [_dot(u[:, p * PAIR:(p + 1) * PAIR], wri_ref[p]) for p in range(A_WIDTH // PAIR)]
    r = _sigmoid(jnp.concatenate([x[:, :PAIR] for x in ri], axis=1) + bri_ref[:, :A_WIDTH])
    i = _sigmoid(jnp.concatenate([x[:, PAIR:] for x in ri], axis=1) + bri_ref[:, A_WIDTH:])
    yield
    log_a = (-A_C * r) * _softplus(-lam_ref[...])
    a = jnp.exp(log_a)
    om = 1.0 - a * a
    d = (om * lax.rsqrt(jnp.maximum(om, TINY))) * (i * u)
    yield
    rows = lax.broadcasted_iota(jnp.int32, (t, A_WIDTH), 0) & 7
    for s in (1, 2, 4):
        a_sh = jnp.where(rows < s, 1.0, pltpu.roll(a, s, 0))
        d_sh = jnp.where(rows < s, 0.0, pltpu.roll(d, s, 0))
        d = a * d_sh + d
        a = a * a_sh
        yield
    carry = h_a[bi, 0:1, :]
    hs = []
    for g in range(t // 8):
        hg = d[8 * g:8 * g + 8, :] + a[8 * g:8 * g + 8, :] * carry
        carry = hg[7:8, :]
        hs.append(hg)
        if g % 8 == 7:
            yield
    hseq = jnp.concatenate(hs, axis=0)
    h_a[bi, 0:1, :] = carry
    o_ref[bi, :, 0:A_WIDTH] = hseq * jax.nn.gelu(ga, approximate=True)


def _group_c(bi, p_ref, o_ref, lb_ref, cn_ref, tril, bo, bdm, s_c):
    t = CHUNK
    q_raw = p_ref[bi, :, C_OFF:C_OFF + C_WIDTH]
    fz = p_ref[bi, :, C_OFF + C_WIDTH:C_OFF + 2 * C_WIDTH]
    v = p_ref[bi, :, C_OFF + 2 * C_WIDTH:C_OFF + 3 * C_WIDTH]
    og = p_ref[bi, :, C_OFF + 3 * C_WIDTH:C_OFF + 4 * C_WIDTH]
    lb = lb_ref[...]
    f = lb + (1.0 - lb) * _sigmoid(fz)
    g = jnp.log(f)
    kd = 1.0 - f
    q = q_raw * _sigmoid(q_raw)
    yield
    b = _sel_left(tril, g)
    e_lev = [jnp.exp(x) for x in _level_decays(b)]
    b_last = _last_rows(b)
    qe = q * jnp.exp(b)
    kl = kd * jnp.exp(b_last - b)
    g_last = jnp.exp(b_last)
    dq = _head_sums(q * kd, bo)
    yield
    ti = lax.broadcasted_iota(jnp.int32, (t, PAIR), 0)
    lane = lax.broadcasted_iota(jnp.int32, (t, PAIR), 1)
    first = lane < C_HEAD
    sj = lane & (C_HEAD - 1)
    xs = jnp.where(ti > sj, ti ^ sj, 0)
    npair = C_WIDTH // PAIR
    cs = [(j, p) for j in range(SUB) for p in range(npair)]
    sl = lambda x, c: x[_rows(c[0]), c[1] * PAIR:(c[1] + 1) * PAIR]
    att = [jnp.zeros((t, PAIR), F32) for _ in cs]
    for l in range(C_LEVELS):
        el = e_lev[l]
        ql = q * el
        kl_l = kd * el
        lev = (xs >> l) == 1
        att = [att[i] + jnp.where(lev, _dot(sl(ql, c), _bd(sl(kl_l, c), first), 1, 1), 0.0)
               for i, c in enumerate(cs)]
        yield
    intra = [_dot(att[i], _bd(sl(v, c), first)) + sl(dq, c) * sl(v, c) for i, c in enumerate(cs)]
    upd = [jnp.where(bdm, _dot(sl(v, c), sl(kl, c), 0, 0), 0.0) for c in cs]
    yield
    sts = {}
    for p in range(npair):
        st = s_c[bi, p]
        for j in range(SUB):
            i = cs.index((j, p))
            sts[i] = st
            st = st * g_last[(j + 1) * t - 1:(j + 1) * t, p * PAIR:(p + 1) * PAIR] + upd[i]
        s_c[bi, p] = st
    outs = [intra[i] + _dot(sl(qe, c), sts[i], 1, 1) for i, c in enumerate(cs)]
    o = jnp.concatenate([jnp.concatenate([outs[cs.index((j, p))] for p in range(npair)], axis=1)
                         for j in range(SUB)], axis=0)
    yield
    ms = _head_sums(o * o, bo) * (1.0 / C_HEAD)
    o_ref[bi, :, A_WIDTH + B_WIDTH:D_MIX] = o * lax.rsqrt(ms + NORM_EPS) * cn_ref[...] * (og * _sigmoid(og))


def _group_b(bi, p_ref, o_ref, vf_ref, vfo_ref, mu_ref, wl_ref, w0_ref, a0_ref, kk_ref, ka_ref, rk_ref,
             lng_ref, lnb_ref, v0_ref, v1_ref, v2_ref, tril, bo, bdm, pbuf, s_b):
    t = CHUNK
    w = B_WIDTH
    pb = p_ref[bi, :, B_OFF:B_OFF + B_COLS]
    pbuf[bi, 8:8 + STEP, :] = pb
    prev = pbuf[bi, 7:7 + STEP, :]
    pbuf[bi, 0:8, :] = pbuf[bi, STEP:STEP + 8, :]
    sh = pb + mu_ref[...] * (prev - pb)
    r = sh[:, 0:w]
    k = sh[:, w:2 * w]
    v = sh[:, 2 * w:3 * w]
    lo = sh[:, 3 * w:3 * w + B_LORA]
    lane = lax.broadcasted_iota(jnp.int32, (STEP, B_LORA), 1)
    xl = jnp.where(lane < 32, jnp.tanh(lo), jnp.where(lane < 64, lo, _sigmoid(lo)))
    lora = _dot(xl, wl_ref[...])
    lw = -float(np.exp(-0.5)) * _sigmoid(w0_ref[...] + lora[:, 0:w])
    if vf_ref is None:
        vfo_ref[bi] = v
    else:
        mix = _sigmoid(v0_ref[...] + _dot(_dot(v, v1_ref[...]), v2_ref[...]))
        v = v + (vf_ref[bi] - v) * mix
    aa = _sigmoid(a0_ref[...] + lora[:, w:2 * w])
    gate = lora[:, 2 * w:3 * w]
    kk = k * kk_ref[...]
    k = k * (1.0 + (aa - 1.0) * ka_ref[...])
    sums = _head_sums(jnp.concatenate([kk * kk, r * k * rk_ref[...]], axis=0), bo)
    kk = kk * lax.rsqrt(jnp.maximum(sums[:STEP], KK_NORM_FLOOR ** 2))
    rk = sums[STEP:]
    ap = -kk
    bb = kk * aa
    c = _sel_left(tril, lw)
    c_last = _last_rows(c)
    a_h = ap * jnp.exp(c - lw)
    r_h = r * jnp.exp(c)
    e_inv = jnp.exp(-c)
    b_h = bb * e_inv
    k_h = k * e_inv
    e_cl = jnp.exp(c_last - c)
    b_c = bb * e_cl
    k_c = k * e_cl
    gam = jnp.exp(c_last)
    yield

    ti = lax.broadcasted_iota(jnp.int32, (t, PAIR), 0)
    lane = lax.broadcasted_iota(jnp.int32, (t, PAIR), 1)
    first = lane < B_HEAD
    sj = lane & (B_HEAD - 1)
    strict = ti > sj
    lower = ti >= sj
    blk = (ti >> 4) == (sj >> 4)
    eye = jnp.where(ti == sj, 1.0, 0.0)
    bd = lambda y: _bd(y, first)
    cat0 = lambda xs: jnp.concatenate(xs, axis=0)
    cat1 = lambda xs: jnp.concatenate(xs, axis=1)
    npair = B_WIDTH // PAIR
    cs = [(j, p) for j in range(SUB) for p in range(npair)]
    n = range(len(cs))
    sl = lambda x, i: x[_rows(cs[i][0]), cs[i][1] * PAIR:(cs[i][1] + 1) * PAIR]
    vbd = [bd(sl(v, i)) for i in n]
    ar = [cat0([sl(a_h, i), sl(r_h, i)]) for i in n]
    mm = [_dot(ar[i], cat0([bd(sl(b_h, i)), bd(sl(k_h, i))]), 1, 1) for i in n]
    yield
    m_ab = [jnp.where(strict, m[:t, :PAIR], 0.0) for m in mm]
    m_ak = [jnp.where(strict, m[:t, PAIR:], 0.0) for m in mm]
    n_rb = [jnp.where(lower, m[t:, :PAIR], 0.0) for m in mm]
    n_rk = [jnp.where(lower, m[t:, PAIR:], 0.0) for m in mm]
    dg = [jnp.where(blk, m, 0.0) for m in m_ab]
    off = [m_ab[i] - dg[i] for i in n]
    akv = [_dot(m_ak[i], vbd[i]) for i in n]
    nkv = [_dot(n_rk[i], vbd[i]) for i in n]
    d2 = [_dot(dg[i], bd(dg[i])) for i in n]
    yield
    s4 = [_dot(cat0([d2[i], dg[i]]), bd(d2[i])) for i in n]
    yield
    d4 = [x[:t] for x in s4]
    x1 = [eye + dg[i] + d2[i] + s4[i][t:] for i in n]
    s8 = [_dot(cat0([d4[i], x1[i]]), bd(d4[i])) for i in n]
    yield
    x2 = [x1[i] + s8[i][t:] for i in n]
    dinv = [x2[i] + _dot(x2[i], bd(s8[i][:t])) for i in n]
    yield
    gm = [_dot(dinv[i], bd(off[i])) for i in n]
    yield
    g2 = [_dot(gm[i], bd(gm[i])) for i in n]
    yield
    gs = [eye + gm[i] + g2[i] + _dot(gm[i], bd(g2[i])) for i in n]
    yield
    tinv = [_dot(gs[i], bd(dinv[i])) for i in n]
    yield
    pq = [_dot(tinv[i], cat1([bd(sl(a_h, i)), bd(akv[i])])) for i in n]
    yield
    npq = [_dot(n_rb[i], cat1([bd(pq[i][:, :PAIR]), bd(pq[i][:, PAIR:])])) for i in n]
    yield
    p2 = [sl(r_h, i) + npq[i][:, :PAIR] for i in n]
    q2 = [npq[i][:, PAIR:] + nkv[i] for i in n]
    wp = [jnp.where(bdm, _dot(pq[i][:, :PAIR], sl(b_c, i), 0, 0), 0.0) for i in n]
    zz = [jnp.where(bdm, _dot(cat0([pq[i][:, PAIR:], sl(v, i)]), cat0([sl(b_c, i), sl(k_c, i)]), 0, 0), 0.0)
          for i in n]
    yield
    sts = {}
    for p in range(npair):
        st = s_b[bi, p]
        for j in range(SUB):
            i = cs.index((j, p))
            sts[i] = st
            st = st * gam[(j + 1) * t - 1:(j + 1) * t, p * PAIR:(p + 1) * PAIR] + _dot(st, wp[i]) + zz[i]
        s_b[bi, p] = st
    ys = [_dot(p2[i], sts[i], 1, 1) + q2[i] for i in n]
    yield
    y = cat0([cat1([ys[cs.index((j, p))] for p in range(npair)]) for j in range(SUB)])
    inv_n = 1.0 / B_HEAD
    mean = _head_sums(y, bo) * inv_n
    yc = y - mean
    var = _head_sums(yc * yc, bo) * inv_n
    y = yc * lax.rsqrt(var + B_GN_EPS) * lng_ref[...] + lnb_ref[...]
    y = y + rk * v
    o_ref[bi, :, A_WIDTH:A_WIDTH + B_WIDTH] = y * gate


def _mixer_kernel(layer0, *refs):
    it = iter(refs)
    p_ref = next(it)
    vf_ref = None if layer0 else next(it)
    cw, cb, wri, bri, lam = (next(it) for _ in range(5))
    mu, wl, w0, a0, kkp, kap, rkp, lng, lnb = (next(it) for _ in range(9))
    v0 = v1 = v2 = None
    if not layer0:
        v0, v1, v2 = (next(it) for _ in range(3))
    lb, cn, tril_ref, bo_ref = (next(it) for _ in range(4))
    y_ref = next(it)
    vfo_ref = next(it) if layer0 else None
    xbuf, h_a, pbuf, s_b, s_c = (next(it) for _ in range(5))

    @pl.when(pl.program_id(1) == 0)
    def _():
        xbuf[:, 0:8, :] = jnp.zeros((NBATCH, 8, A_WIDTH), F32)
        pbuf[:, 0:8, :] = jnp.zeros((NBATCH, 8, B_COLS), F32)
        h_a[...] = jnp.zeros_like(h_a)
        s_b[...] = jnp.zeros_like(s_b)
        s_c[...] = jnp.zeros_like(s_c)

    tril = tril_ref[...]
    bo = bo_ref[...]
    half = lambda d: lax.broadcasted_iota(jnp.int32, (PAIR, PAIR), d) < PAIR // 2
    bdm = half(0) == half(1)
    live = []
    for bi in range(NBATCH):
        gb = _group_b(bi, p_ref, y_ref, vf_ref, vfo_ref, mu, wl, w0, a0, kkp, kap, rkp, lng, lnb, v0, v1, v2,
                      tril, bo, bdm, pbuf, s_b)
        gc = _group_c(bi, p_ref, y_ref, lb, cn, tril, bo, bdm, s_c)
        ga = _group_a(bi, p_ref, y_ref, cw, cb, wri, bri, lam, xbuf, h_a)
        next(gb)
        live += [gb, gc, ga]
        rounds = STAGGER if bi + 1 < NBATCH else 1 << 30
        while live and rounds:
            rounds -= 1
            for gen in list(live):
                try:
                    next(gen)
                except StopIteration:
                    live.remove(gen)


def _step_constants():
    eye = np.eye(SUB, dtype=np.float32)
    tril = np.kron(eye, np.tril(np.ones((CHUNK, CHUNK), np.float32)))
    hid = np.arange(PAIR) // B_HEAD
    bo = (hid[:, None] == hid[None, :]).astype(np.float32)
    return tuple(jnp.asarray(m, BF16) for m in (tril, bo))


def _mixer(proj, v_first, params, consts, layer0):
    bsz, lp, _ = proj.shape
    t = STEP
    const = lambda b, c: (0, 0)
    row = lambda b, c: (b, c, 0)
    ins = [proj]
    specs = [pl.BlockSpec((NBATCH, t, N_IN), row)]
    if not layer0:
        ins.append(v_first)
        specs.append(pl.BlockSpec((NBATCH, t, B_WIDTH), row))
    for arr, layer in params:
        ins.append(arr)
        specs.append(_layer_spec(arr, layer))
    for arr in consts:
        ins.append(arr)
        specs.append(pl.BlockSpec(arr.shape, const))
    out_shape = [jax.ShapeDtypeStruct((bsz, lp, D_MIX), F32)]
    out_specs = [pl.BlockSpec((NBATCH, t, D_MIX), row)]
    if layer0:
        out_shape.append(jax.ShapeDtypeStruct((bsz, lp, B_WIDTH), F32))
        out_specs.append(pl.BlockSpec((NBATCH, t, B_WIDTH), row))
    res = pl.pallas_call(
        functools.partial(_mixer_kernel, layer0),
        grid=(bsz // NBATCH, lp // t),
        in_specs=specs,
        out_specs=out_specs,
        out_shape=out_shape,
        scratch_shapes=[pltpu.VMEM((NBATCH, t + 8, A_WIDTH), F32),
                        pltpu.VMEM((NBATCH, 8, A_WIDTH), F32),
                        pltpu.VMEM((NBATCH, t + 8, B_COLS), F32),
                        pltpu.VMEM((NBATCH, B_WIDTH // PAIR, PAIR, PAIR), F32),
                        pltpu.VMEM((NBATCH, C_WIDTH // PAIR, PAIR, PAIR), F32)],
        compiler_params=pltpu.CompilerParams(dimension_semantics=("arbitrary", "arbitrary"),
                                             vmem_limit_bytes=V7X_VMEM_LIMIT),
        name="mixer0" if layer0 else "mixer",
    )(*ins)
    return res if layer0 else (res[0], v_first)


def _block_diag(w):
    nl, hn, n, _ = w.shape
    return jnp.concatenate([jnp.pad(w[:, h], ((0, 0), (0, 0), (h * n, (hn - 1 - h) * n))) for h in range(hn)],
                           axis=1)


def kernel(x, meta, mix_norm, w_in, a_conv_w, a_conv_b, a_w_r, a_b_r, a_w_i, a_b_i, a_lambda, b_mu, b_w0, b_w2, b_a0, b_a2, b_g2, b_k_k, b_k_a, b_r_k, b_ln_g, b_ln_b, b_v0, b_v1, b_v2, c_lb, c_norm, w_out, ffn_norm, w_up, w_down, final_norm):
    bsz, seq, d = x.shape
    depth = w_in.shape[0]
    l_real = N_META + seq
    lp = -(-l_real // STEP) * STEP
    while (bsz * lp) % ROW_TILE:
        lp += STEP
    lb_all = jnp.cumsum(jax.nn.softmax(c_lb.astype(F32), axis=0), axis=0)
    lb_all = lb_all - lb_all[0]
    consts = _step_constants()
    row = lambda a: a.reshape(a.shape[0], 1, -1)
    pair_bd = lambda w: _block_diag(w.reshape(depth * (A_HEADS // 2), 2, *w.shape[2:]))
    wri = jnp.concatenate([pair_bd(a_w_r), pair_bd(a_w_i)], axis=2).astype(BF16)
    wri = wri.reshape(depth, A_HEADS // 2, PAIR, 2 * PAIR)
    bri = row(jnp.concatenate([a_b_r, a_b_i], axis=1))
    cols = lambda a, i: jnp.pad(a, ((0, 0), (0, 0), (i * B_WIDTH, (2 - i) * B_WIDTH)))
    wl = jnp.concatenate([cols(b_w2, 0), cols(b_a2, 1), cols(b_g2, 2)], axis=1).astype(BF16)
    shared = [a_conv_w, row(a_conv_b), wri, bri, row(a_lambda), row(b_mu), wl, row(b_w0), row(b_a0),
              row(b_k_k), row(b_k_a), row(b_r_k), row(b_ln_g), row(b_ln_b)]
    vmix = [row(b_v0), b_v1.astype(BF16), b_v2.astype(BF16)]
    tail = [row(lb_all), row(jnp.tile(c_norm, (1, C_HEADS)))]
    g_mix, g_ffn = row(mix_norm), row(ffn_norm)

    v_first = None
    for l in range(depth):
        if l == 0:
            proj, h, w_up_b, w_down_b = _proj_first(x, meta.astype(x.dtype), g_mix, w_in, w_up, w_down, lp)
            h = h.reshape(bsz * lp, d)
        else:
            proj, w_up_b, w_down_b = _proj(h, g_mix, w_in, w_up, w_down, l)
            proj = proj.reshape(bsz, lp, N_IN)
        params = [(a, l) for a in shared] + ([(a, l - 1) for a in vmix] if l else []) + [(a, l) for a in tail]
        y, v_first = _mixer(proj, v_first, params, consts, l == 0)
        if l < depth - 1:
            h = _post(h, y.reshape(bsz * lp, D_MIX), w_out, g_ffn, w_up_b, w_down_b, l)
    return _post_last(h.reshape(bsz, lp, d), y, w_out, g_ffn, w_up_b, w_down_b,
                      final_norm.reshape(1, -1), depth - 1, seq)
```
